```python
import jax, jax.numpy as jnp
from jax import lax
import numpy as np

D_MODEL = 1024
BATCH = 16
SEQ = 4096
DEPTH = 1
DEC_BATCH = 16
DEC_SEQ = 64
PAST_LEN = 2048

CHUNK = 64
Q_BLOCK = 128
EPS = 1e-6
M_HEADS = 4
M_HEAD_DIM = 128
M_WIDTH = M_HEADS * M_HEAD_DIM
A_HEADS = 8
A_NOPE = 64
A_ROPE = 32
A_V = 64
A_WIDTH = A_HEADS * A_V
Q_LORA = 384
KV_LORA = 256
ROPE_THETA = 10000.0
A_SCALE = (A_NOPE + A_ROPE) ** -0.5
MIX_WIDTH = M_WIDTH + A_WIDTH
IN_WIDTH = 4 * M_WIDTH + 2 * M_HEADS + Q_LORA + KV_LORA + A_ROPE
D_FF = 2816

kernel_name = 'hybrid_mlstm_mla_stream_step'


def rms_norm(x, g):
    x32 = x.astype(jnp.float32)
    y = x32 * lax.rsqrt(jnp.mean(x32 * x32, axis=-1, keepdims=True) + EPS)
    return (y * g.astype(jnp.float32)).astype(x.dtype)


def swiglu_half(x, pre_g, w_gate, w_up, w_down, post_g):
    h = rms_norm(x, pre_g)
    u = jax.nn.silu(h @ w_gate) * (h @ w_up)
    return x + 0.5 * rms_norm(u @ w_down, post_g)


def rope(x, pos):
    half = A_ROPE // 2
    freq = ROPE_THETA ** (-jnp.arange(half, dtype=jnp.float32) / half)
    ang = pos.astype(jnp.float32)[:, None] * freq[None, :]
    shape = (ang.shape[0],) + (1,) * (x.ndim - 3) + (half,)
    cos = jnp.cos(ang).reshape(shape)
    sin = jnp.sin(ang).reshape(shape)
    x32 = x.astype(jnp.float32)
    x1, x2 = x32[..., :half], x32[..., half:]
    return jnp.concatenate([x1 * cos - x2 * sin, x2 * cos + x1 * sin], axis=-1).astype(x.dtype)


def mixer_projections(h, pos, w_in, b_igate, b_fgate, q_a_g, w_uq, kv_a_g):
    B, S = h.shape[:2]
    sizes = (M_WIDTH, M_WIDTH, M_WIDTH, M_WIDTH, M_HEADS, M_HEADS, Q_LORA, KV_LORA, A_ROPE)
    idx = [int(i) for i in np.cumsum(sizes)[:-1]]
    mq, mk, mv, mo, mi, mf, aq, akv, ar = jnp.split(h @ w_in, idx, axis=-1)
    heads = lambda a: a.reshape(B, S, M_HEADS, M_HEAD_DIM)
    mq, mv, mo = heads(mq), heads(mv), heads(mo)
    mk = heads(mk) * (M_HEAD_DIM ** -0.5)
    ig = (mi + b_igate).astype(jnp.float32)
    lf = jax.nn.log_sigmoid((mf + b_fgate).astype(jnp.float32))
    q = (rms_norm(aq, q_a_g) @ w_uq).reshape(B, S, A_HEADS, A_NOPE + A_ROPE)
    q_nope, q_rope = q[..., :A_NOPE], rope(q[..., A_NOPE:], pos)
    c_kv = rms_norm(akv, kv_a_g)
    k_rope = rope(ar, pos)
    return mo, (mq, mk, mv, ig, lf, q_nope, q_rope, c_kv, k_rope)


def mlstm_chunk(carry, inp):
    C, n, m = carry
    q, k, v, ig, lf = inp
    q, k, v = q.astype(jnp.float32), k.astype(jnp.float32), v.astype(jnp.float32)
    ig = jnp.moveaxis(ig.astype(jnp.float32), 1, 2)
    lf = jnp.moveaxis(lf.astype(jnp.float32), 1, 2)
    L = q.shape[1]
    b = jnp.cumsum(lf, axis=-1)
    causal = jnp.tril(jnp.ones((L, L), dtype=bool))
    log_d = jnp.where(causal, b[..., :, None] - b[..., None, :] + ig[..., None, :], -jnp.inf)
    log_carry = b + m[..., None]
    m_t = jnp.maximum(log_carry, jnp.max(log_d, axis=-1))
    d = jnp.exp(log_d - m_t[..., None])
    w_carry = jnp.exp(log_carry - m_t)
    s = jnp.einsum('bthd,bshd->bhts', q, k) * d
    num = jnp.einsum('bhts,bshd->bhtd', s, v) + w_carry[..., None] * jnp.einsum('bhvk,bthk->bhtv', C, q)
    den = jnp.sum(s, axis=-1) + w_carry * jnp.einsum('bhk,bthk->bht', n, q)
    den = jnp.maximum(jnp.abs(den), jnp.exp(-m_t))
    h = jnp.moveaxis(num / den[..., None], 1, 2)
    m_new = m_t[..., -1]
    w_state = jnp.exp(b[..., -1] + m - m_new)
    w_row = jnp.exp(b[..., -1:] - b + ig - m_new[..., None])
    C_new = w_state[..., None, None] * C + jnp.einsum('bhs,bshv,bshk->bhvk', w_row, v, k)
    n_new = w_state[..., None] * n + jnp.einsum('bhs,bshk->bhk', w_row, k)
    return (C_new, n_new, m_new), h


def mla_expand(c_kv, w_ukv):
    B, T = c_kv.shape[:2]
    kv = (c_kv @ w_ukv).reshape(B, T, A_HEADS, A_NOPE + A_V)
    return kv[..., :A_NOPE], kv[..., A_NOPE:]


def mla_attend(q_nope, q_rope, k_nope, k_rope, v, q_pos, k_pos):
    s = jnp.einsum('bqhd,bkhd->bhqk', q_nope, k_nope) + jnp.einsum('bqhr,bkr->bhqk', q_rope, k_rope)
    s = s.astype(jnp.float32) * A_SCALE
    visible = (q_pos[:, None] // CHUNK) >= (k_pos[None, :] // CHUNK)
    s = jnp.where(visible, s, -jnp.inf)
    p = jax.nn.softmax(s, axis=-1).astype(v.dtype)
    return jnp.einsum('bhqk,bkhd->bqhd', p, v)


def prompt_token_mix(proj, pos, w_ukv):
    mq, mk, mv, ig, lf, q_nope, q_rope, c_kv, k_rope = proj
    B, S = mq.shape[:2]
    nc = S // CHUNK
    to_chunks = lambda a: jnp.moveaxis(a.reshape((B, nc, CHUNK) + a.shape[2:]), 1, 0)
    init = (jnp.zeros((B, M_HEADS, M_HEAD_DIM, M_HEAD_DIM), jnp.float32),
            jnp.zeros((B, M_HEADS, M_HEAD_DIM), jnp.float32),
            jnp.zeros((B, M_HEADS), jnp.float32))
    (C, n, m), hs = lax.scan(mlstm_chunk, init, (to_chunks(mq), to_chunks(mk), to_chunks(mv), to_chunks(ig), to_chunks(lf)))
    h_m = jnp.moveaxis(hs, 0, 1).reshape(B, S, M_HEADS, M_HEAD_DIM)
    k_nope, v = mla_expand(c_kv, w_ukv)
    nb = S // Q_BLOCK
    to_qblocks = lambda a: jnp.moveaxis(a.reshape((B, nb, Q_BLOCK) + a.shape[2:]), 1, 0)
    attn_blocks = lax.map(lambda blk: mla_attend(blk[0], blk[1], k_nope, k_rope, v, blk[2], pos),
                          (to_qblocks(q_nope), to_qblocks(q_rope), pos.reshape(nb, Q_BLOCK)))
    attn = jnp.moveaxis(attn_blocks, 0, 1).reshape(B, S, A_HEADS, A_V)
    return h_m, attn, (c_kv, k_rope, C, n, m)


def sample_token_mix(proj, pos, w_ukv, cache_kv_latent, cache_k_rope, state_C, state_n, state_m):
    mq, mk, mv, ig, lf, q_nope, q_rope, c_kv, k_rope = proj
    carry = (state_C.astype(jnp.float32), state_n.astype(jnp.float32), state_m.astype(jnp.float32))
    (C, n, m), h_m = mlstm_chunk(carry, (mq, mk, mv, ig, lf))
    c_all = jnp.concatenate([cache_kv_latent.astype(c_kv.dtype), c_kv], axis=1)
    kr_all = jnp.concatenate([cache_k_rope.astype(k_rope.dtype), k_rope], axis=1)
    k_nope, v = mla_expand(c_all, w_ukv)
    k_pos = jnp.arange(c_all.shape[1])
    attn = mla_attend(q_nope, q_rope, k_nope, kr_all, v, pos, k_pos)
    return h_m, attn, (c_kv, k_rope, C, n, m)


def run_layer(x, pos, token_mix, ff1, mix, ff2):
    x = swiglu_half(x, *ff1)
    mix_pre_g, w_in, b_igate, b_fgate, q_a_g, w_uq, kv_a_g, m_head_g, w_out, mix_post_g = mix
    mo, proj = mixer_projections(rms_norm(x, mix_pre_g), pos, w_in, b_igate, b_fgate, q_a_g, w_uq, kv_a_g)
    h_m, attn, new_state = token_mix(proj)
    B, S = x.shape[:2]
    h_m = rms_norm(h_m.astype(x.dtype), m_head_g) * jax.nn.sigmoid(mo)
    merged = jnp.concatenate([h_m.reshape(B, S, M_WIDTH), attn.reshape(B, S, A_WIDTH)], axis=-1)
    x = x + rms_norm(merged @ w_out, mix_post_g)
    x = swiglu_half(x, *ff2)
    return x, new_state


def setup_inputs(seed: int = 0) -> dict:
    key = jax.random.key(seed)
    ks = iter(jax.random.split(key, 40))
    f32 = jnp.float32
    nrm = lambda shape, scale: scale * jax.random.normal(next(ks), shape, f32)
    gain = lambda shape: 1.0 + 0.05 * jax.random.normal(next(ks), shape, f32)
    L = DEPTH
    return {
        'x_prompt': nrm((BATCH, SEQ, D_MODEL), 1.0),
        'x_sample': nrm((DEC_BATCH, DEC_SEQ, D_MODEL), 1.0),
        'cache_kv_latent': nrm((L, DEC_BATCH, PAST_LEN, KV_LORA), 1.0),
        'cache_k_rope': nrm((L, DEC_BATCH, PAST_LEN, A_ROPE), 1.0),
        'state_C': nrm((L, DEC_BATCH, M_HEADS, M_HEAD_DIM, M_HEAD_DIM), M_HEAD_DIM ** -0.5),
        'state_n': nrm((L, DEC_BATCH, M_HEADS, M_HEAD_DIM), 0.5),
        'state_m': nrm((L, DEC_BATCH, M_HEADS), 1.0),
        'ff1_pre_g': gain((L, D_MODEL)),
        'ff1_w_gate': nrm((L, D_MODEL, D_FF), D_MODEL ** -0.5),
        'ff1_w_up': nrm((L, D_MODEL, D_FF), D_MODEL ** -0.5),
        'ff1_w_down': nrm((L, D_FF, D_MODEL), D_FF ** -0.5),
        'ff1_post_g': gain((L, D_MODEL)),
        'mix_pre_g': gain((L, D_MODEL)),
        'w_in': nrm((L, D_MODEL, IN_WIDTH), D_MODEL ** -0.5),
        'b_igate': nrm((L, M_HEADS), 0.1),
        'b_fgate': 3.0 + nrm((L, M_HEADS), 0.5),
        'q_a_g': gain((L, Q_LORA)),
        'w_uq': nrm((L, Q_LORA, A_HEADS * (A_NOPE + A_ROPE)), Q_LORA ** -0.5),
        'kv_a_g': gain((L, KV_LORA)),
        'w_ukv': nrm((L, KV_LORA, A_HEADS * (A_NOPE + A_V)), KV_LORA ** -0.5),
        'm_head_g': gain((L, M_HEADS, M_HEAD_DIM)),
        'w_out': nrm((L, MIX_WIDTH, D_MODEL), MIX_WIDTH ** -0.5),
        'mix_post_g': gain((L, D_MODEL)),
        'ff2_pre_g': gain((L, D_MODEL)),
        'ff2_w_gate': nrm((L, D_MODEL, D_FF), D_MODEL ** -0.5),
        'ff2_w_up': nrm((L, D_MODEL, D_FF), D_MODEL ** -0.5),
        'ff2_w_down': nrm((L, D_FF, D_MODEL), D_FF ** -0.5),
        'ff2_post_g': gain((L, D_MODEL)),
    }


def reference(x_prompt, x_sample, cache_kv_latent, cache_k_rope, state_C, state_n, state_m,
              ff1_pre_g, ff1_w_gate, ff1_w_up, ff1_w_down, ff1_post_g,
              mix_pre_g, w_in, b_igate, b_fgate, q_a_g, w_uq, kv_a_g, w_ukv, m_head_g, w_out, mix_post_g,
              ff2_pre_g, ff2_w_gate, ff2_w_up, ff2_w_down, ff2_post_g):
    y_prompt, y_sample = x_prompt, x_sample
    pos_p = jnp.arange(x_prompt.shape[1])
    pos_s = cache_kv_latent.shape[2] + jnp.arange(x_sample.shape[1])
    new_p, new_s = [], []
    for l in range(DEPTH):
        ff1 = (ff1_pre_g[l], ff1_w_gate[l], ff1_w_up[l], ff1_w_down[l], ff1_post_g[l])
        mix = (mix_pre_g[l], w_in[l], b_igate[l], b_fgate[l], q_a_g[l], w_uq[l], kv_a_g[l],
               m_head_g[l], w_out[l], mix_post_g[l])
        ff2 = (ff2_pre_g[l], ff2_w_gate[l], ff2_w_up[l], ff2_w_down[l], ff2_post_g[l])
        wkv = w_ukv[l]
        y_prompt, st_p = run_layer(y_prompt, pos_p, lambda proj: prompt_token_mix(proj, pos_p, wkv), ff1, mix, ff2)
        cache_l = (cache_kv_latent[l], cache_k_rope[l], state_C[l], state_n[l], state_m[l])
        y_sample, st_s = run_layer(y_sample, pos_s, lambda proj: sample_token_mix(proj, pos_s, wkv, *cache_l), ff1, mix, ff2)
        new_p.append(st_p)
        new_s.append(st_s)
    p_kv_latent, p_k_rope, p_C, p_n, p_m = [jnp.stack(a) for a in zip(*new_p)]
    s_kv_latent, s_k_rope, s_C, s_n, s_m = [jnp.stack(a) for a in zip(*new_s)]
    return (y_prompt, y_sample, p_kv_latent, p_k_rope, p_C, p_n, p_m, s_kv_latent, s_k_rope, s_C, s_n, s_m)
```

```python
import functools

import numpy as np
import jax
import jax.numpy as jnp
from jax import lax
from jax.experimental import pallas as pl
from jax.experimental.pallas import tpu as pltpu

F32 = jnp.float32
BF16 = jnp.bfloat16

D_MODEL = 1024
CHUNK = 64
CHUNK_SHIFT = 6
EPS = 1e-6
M_HEADS = 4
M_HEAD_DIM = 128
M_WIDTH = M_HEADS * M_HEAD_DIM
A_HEADS = 8
A_NOPE = 64
A_ROPE = 32
A_V = 64
A_WIDTH = A_HEADS * A_V
Q_LORA = 384
KV_LORA = 256
ROPE_THETA = 10000.0
A_SCALE = (A_NOPE + A_ROPE) ** -0.5
D_FF = 2816

LANES = 128
A_KPAD = A_HEADS * LANES
COL_MQ, COL_MK, COL_MV, COL_MO = 0, M_WIDTH, 2 * M_WIDTH, 3 * M_WIDTH
COL_AQ = 4 * M_WIDTH
COL_AKV = COL_AQ + Q_LORA
COL_AUX = COL_AKV + KV_LORA
IN_PAD = COL_AUX + LANES
AUX_ROPE_LANE = A_NOPE
VMEM_LIMIT = 56 * 1024 * 1024


def _rms(x, g):
    return x * lax.rsqrt(jnp.mean(x * x, axis=-1, keepdims=True) + EPS) * g


def _const_spec(shape):
    nd = len(shape)
    return pl.BlockSpec(shape, lambda *_: (0,) * nd, pipeline_mode=pl.Buffered(1))


def _params(sem):
    return pltpu.CompilerParams(dimension_semantics=sem, vmem_limit_bytes=VMEM_LIMIT)


def _row_tile(n, cap):
    best = None
    for t in range(16, min(n, cap) + 1, 16):
        if n % t == 0:
            best = t
    assert best is not None, n
    return best


def _ffn_kernel(x_ref, pre_g, wg, wu, wd, post_g, o_ref):
    x = x_ref[...]
    h = _rms(x, pre_g[...]).astype(BF16)
    g = jnp.dot(h, wg[...], preferred_element_type=F32)
    u = jnp.dot(h, wu[...], preferred_element_type=F32)
    a = (g * jax.nn.sigmoid(g) * u).astype(BF16)
    y = jnp.dot(a, wd[...], preferred_element_type=F32)
    o_ref[...] = x + 0.5 * _rms(y, post_g[...])


def _ffn(x, pre_g, wg, wu, wd, post_g):
    n = x.shape[0]
    tm = _row_tile(n, 512)
    return pl.pallas_call(
        _ffn_kernel,
        grid=(n // tm,),
        in_specs=[pl.BlockSpec((tm, D_MODEL), lambda i: (i, 0)),
                  _const_spec((1, D_MODEL)), _const_spec((D_MODEL, D_FF)),
                  _const_spec((D_MODEL, D_FF)), _const_spec((D_FF, D_MODEL)),
                  _const_spec((1, D_MODEL))],
        out_specs=pl.BlockSpec((tm, D_MODEL), lambda i: (i, 0)),
        out_shape=jax.ShapeDtypeStruct((n, D_MODEL), F32),
        compiler_params=_params(("parallel",)),
        name="ffn",
    )(x, pre_g, wg, wu, wd, post_g)


def _rope_lanes(x, cos, sin_up, sin_dn):
    half = A_ROPE // 2
    return (x * cos + pltpu.roll(x, half, axis=1) * sin_up
            + pltpu.roll(x, LANES - half, axis=1) * sin_dn)


def _proj_kernel(x_ref, pre_g, w_in, q_a_g, w_uq, kv_a_g, cos_ref, sup_ref, sdn_ref,
                 mq_ref, mk_ref, mv_ref, mo_ref, aux_ref, grow_ref, q_ref, ckv_ref):
    h = _rms(x_ref[...], pre_g[...]).astype(BF16)
    p = jnp.dot(h, w_in[...], preferred_element_type=F32)
    mq_ref[...] = p[:, COL_MQ:COL_MQ + M_WIDTH].astype(BF16)
    mk_ref[...] = (p[:, COL_MK:COL_MK + M_WIDTH] * (M_HEAD_DIM ** -0.5)).astype(BF16)
    mv_ref[...] = p[:, COL_MV:COL_MV + M_WIDTH].astype(BF16)
    mo_ref[...] = p[:, COL_MO:COL_MO + M_WIDTH].astype(BF16)
    cos, sup, sdn = cos_ref[...], sup_ref[...], sdn_ref[...]
    aux = _rope_lanes(p[:, COL_AUX:COL_AUX + LANES], cos, sup, sdn)
    aux_ref[...] = aux
    grow_ref[...] = aux.T[:2 * M_HEADS, :]
    ckv_ref[...] = _rms(p[:, COL_AKV:COL_AKV + KV_LORA], kv_a_g[...])
    qa = _rms(p[:, COL_AQ:COL_AQ + Q_LORA], q_a_g[...]).astype(BF16)
    q = jnp.dot(qa, w_uq[...], preferred_element_type=F32)
    for hd in range(A_HEADS):
        sl = slice(hd * LANES, (hd + 1) * LANES)
        q_ref[:, sl] = (_rope_lanes(q[:, sl], cos, sup, sdn) * A_SCALE).astype(BF16)


def _proj(x, tables, pre_g, w_in, q_a_g, w_uq, kv_a_g):
    b, s, _ = x.shape
    tm = _row_tile(s, 512)
    tok = lambda w: pl.BlockSpec((None, tm, w), lambda bi, j: (bi, j, 0))
    tab = pl.BlockSpec((tm, LANES), lambda bi, j: (j, 0))
    out_shape = ([jax.ShapeDtypeStruct((b, s, M_WIDTH), BF16)] * 4
                 + [jax.ShapeDtypeStruct((b, s, LANES), F32),
                    jax.ShapeDtypeStruct((b, 2 * M_HEADS, s), F32),
                    jax.ShapeDtypeStruct((b, s, A_KPAD), BF16),
                    jax.ShapeDtypeStruct((b, s, KV_LORA), F32)])
    out_specs = ([tok(M_WIDTH)] * 4
                 + [tok(LANES),
                    pl.BlockSpec((None, 2 * M_HEADS, tm), lambda bi, j: (bi, 0, j)),
                    tok(A_KPAD), tok(KV_LORA)])
    return pl.pallas_call(
        _proj_kernel,
        grid=(b, s // tm),
        in_specs=[tok(D_MODEL), _const_spec((1, D_MODEL)), _const_spec((D_MODEL, IN_PAD)),
                  _const_spec((1, Q_LORA)), _const_spec((Q_LORA, A_KPAD)),
                  _const_spec((1, KV_LORA)), tab, tab, tab],
        out_specs=out_specs,
        out_shape=out_shape,
        compiler_params=_params(("parallel", "parallel")),
        name="proj",
    )(x, pre_g, w_in, q_a_g, w_uq, kv_a_g, *tables)


def _expand_kernel(c_ref, aux_ref, wk, wv, k_ref, v_ref):
    cb = c_ref[...].astype(BF16)
    aux = aux_ref[...]
    lane = lax.broadcasted_iota(jnp.int32, aux.shape, 1)
    k_rope = jnp.where(lane >= AUX_ROPE_LANE, aux, 0.0)
    kn = jnp.dot(cb, wk[...], preferred_element_type=F32)
    for hd in range(A_HEADS):
        sl = slice(hd * LANES, (hd + 1) * LANES)
        k_ref[:, sl] = (kn[:, sl] + k_rope).astype(BF16)
    v_ref[...] = jnp.dot(cb, wv[...], preferred_element_type=F32).astype(BF16)


def _expand(c, aux, wk, wv):
    b, t, _ = c.shape
    tm = _row_tile(t, 1024)
    tok = lambda w: pl.BlockSpec((None, tm, w), lambda bi, j: (bi, j, 0))
    return pl.pallas_call(
        _expand_kernel,
        grid=(b, t // tm),
        in_specs=[tok(KV_LORA), tok(LANES), _const_spec((KV_LORA, A_KPAD)),
                  _const_spec((KV_LORA, A_WIDTH))],
        out_specs=[tok(A_KPAD), tok(A_WIDTH)],
        out_shape=[jax.ShapeDtypeStruct((b, t, A_KPAD), BF16),
                   jax.ShapeDtypeStruct((b, t, A_WIDTH), BF16)],
        compiler_params=_params(("parallel", "parallel")),
        name="expand",
    )(c, aux, wk, wv)


def _log_sigmoid(x):
    return jnp.minimum(x, 0.0) - jnp.log(1.0 + jnp.exp(-jnp.abs(x)))


def _cumsum(x, axis):
    n = x.shape[axis]
    idx = lax.broadcasted_iota(jnp.int32, x.shape, axis)
    k = 1
    while k < n:
        x = x + jnp.where(idx >= k, pltpu.roll(x, k, axis=axis), 0.0)
        k *= 2
    return x


def _mlstm_kernel(mq_ref, mk_ref, mv_ref, mo_ref, aux_ref, grow_ref, bcol_ref, brow_ref, hg_ref,
                  c0_ref, n0_ref, m0_ref, hm_ref, c_ref, n_ref, m_ref):
    @pl.when(pl.program_id(1) == 0)
    def _():
        c_ref[...] = c0_ref[...]
        n_ref[...] = n0_ref[...]
        m_ref[...] = m0_ref[...]

    ln = mq_ref.shape[0]
    pre_c = aux_ref[...] + bcol_ref[...]
    b_c = _cumsum(_log_sigmoid(pre_c), 0)
    pre_r = grow_ref[...] + brow_ref[...]
    b_r = _cumsum(_log_sigmoid(pre_r), 1)
    row = lax.broadcasted_iota(jnp.int32, (ln, ln), 0)
    col = lax.broadcasted_iota(jnp.int32, (ln, ln), 1)
    causal = row >= col
    nt = (((1,), (1,)), ((), ()))
    for hd in range(M_HEADS):
        sl = slice(hd * M_HEAD_DIM, (hd + 1) * M_HEAD_DIM)
        q, k, v = mq_ref[:, sl], mk_ref[:, sl], mv_ref[:, sl]
        m_prev = m_ref[hd][:, :1]
        c_prev = c_ref[hd]
        n_prev = n_ref[hd]
        a_row = pre_r[hd:hd + 1, :] - b_r[M_HEADS + hd:M_HEADS + hd + 1, :]
        b_col = b_c[:, M_HEADS + hd:M_HEADS + hd + 1]
        a_col = pre_c[:, hd:hd + 1] - b_col
        a_mat = jnp.where(causal, a_row, -jnp.inf)
        g = jnp.maximum(m_prev, jnp.max(a_mat, axis=1, keepdims=True))
        d = jnp.exp(a_mat - g)
        s = lax.dot_general(q, k, nt, preferred_element_type=F32) * d
        w_carry = jnp.exp(m_prev - g)
        cq = lax.dot_general(q, c_prev.astype(BF16), nt, preferred_element_type=F32)
        num = jnp.dot(s.astype(BF16), v, preferred_element_type=F32) + w_carry * cq
        nq = jnp.sum(q.astype(F32) * n_prev, axis=1, keepdims=True)
        den = jnp.sum(s, axis=1, keepdims=True) + w_carry * nq
        den = jnp.maximum(jnp.abs(den), jnp.exp(-(b_col + g)))
        hh = num / den
        hn = _rms(hh, hg_ref[:, sl])
        hm_ref[:, sl] = (hn * jax.nn.sigmoid(mo_ref[:, sl].astype(F32))).astype(BF16)
        g_last = g[ln - 1:ln, :]
        w_state = jnp.exp(m_prev - g_last)
        w_col = jnp.exp(a_col - g_last)
        vw_t = (v.astype(F32) * w_col).T.astype(BF16)
        c_ref[hd] = w_state * c_prev + jnp.dot(vw_t, k, preferred_element_type=F32)
        n_ref[hd] = w_state * n_prev + jnp.sum(w_col * k.astype(F32), axis=0, keepdims=True)
        m_ref[hd] = jnp.broadcast_to(b_col[ln - 1:ln, :] + g_last, (1, LANES))


def _mlstm(mq, mk, mv, mo, aux, grow, bias_col, bias_row, head_g, c0, n0, m0, block):
    b, s, _ = mq.shape
    tok = lambda w: pl.BlockSpec((None, block, w), lambda bi, j: (bi, j, 0))
    st_c = pl.BlockSpec((None, M_HEADS, M_HEAD_DIM, M_HEAD_DIM), lambda bi, j: (bi, 0, 0, 0))
    st_v = pl.BlockSpec((None, M_HEADS, 1, M_HEAD_DIM), lambda bi, j: (bi, 0, 0, 0))
    return pl.pallas_call(
        _mlstm_kernel,
        grid=(b, s // block),
        in_specs=[tok(M_WIDTH)] * 4
        + [tok(LANES), pl.BlockSpec((None, 2 * M_HEADS, block), lambda bi, j: (bi, 0, j)),
           _const_spec((1, LANES)), _const_spec((2 * M_HEADS, 1)), _const_spec((1, M_WIDTH)),
           st_c, st_v, st_v],
        out_specs=[tok(M_WIDTH), st_c, st_v, st_v],
        out_shape=[jax.ShapeDtypeStruct((b, s, M_WIDTH), BF16),
                   jax.ShapeDtypeStruct((b, M_HEADS, M_HEAD_DIM, M_HEAD_DIM), F32),
                   jax.ShapeDtypeStruct((b, M_HEADS, 1, M_HEAD_DIM), F32),
                   jax.ShapeDtypeStruct((b, M_HEADS, 1, LANES), F32)],
        compiler_params=_params(("parallel", "arbitrary")),
        name="mlstm",
    )(mq, mk, mv, mo, aux, grow, bias_col, bias_row, head_g, c0, n0, m0)


def _attn_kernel(qi_ref, ki_ref, last_ref, mask_ref, q_ref, k_ref, v_ref, o_ref,
                 m_scr, l_scr, acc_scr, *, q_off):
    t = pl.program_id(1)
    qi, ki = qi_ref[t], ki_ref[t]
    tq, tk = q_ref.shape[0], k_ref.shape[0]

    @pl.when(ki == 0)
    def _():
        m_scr[...] = jnp.full(m_scr.shape, -jnp.inf, F32)
        l_scr[...] = jnp.zeros(l_scr.shape, F32)
        acc_scr[...] = jnp.zeros(acc_scr.shape, F32)

    def step(masked):
        if masked:
            q_chunk = (q_off + qi * tq + lax.broadcasted_iota(jnp.int32, (tq, tk), 0)) >> CHUNK_SHIFT
            k_chunk = (ki * tk + lax.broadcasted_iota(jnp.int32, (tq, tk), 1)) >> CHUNK_SHIFT
            visible = q_chunk >= k_chunk
        for hd in range(A_HEADS):
            sl = slice(hd * LANES, (hd + 1) * LANES)
            s = lax.dot_general(q_ref[:, sl], k_ref[:, sl], (((1,), (1,)), ((), ())),
                                preferred_element_type=F32)
            if masked:
                s = jnp.where(visible, s, -jnp.inf)
            m_prev = m_scr[hd]
            m_new = jnp.maximum(m_prev, jnp.max(s, axis=1, keepdims=True))
            alpha = jnp.exp(m_prev - m_new)
            p = jnp.exp(s - m_new)
            l_scr[hd] = alpha * l_scr[hd] + jnp.sum(p, axis=1, keepdims=True)
            pair = slice((hd // 2) * LANES, (hd // 2 + 1) * LANES)
            acc_scr[hd] = alpha * acc_scr[hd] + jnp.dot(p.astype(BF16), v_ref[:, pair],
                                                        preferred_element_type=F32)
            m_scr[hd] = m_new

    pl.when(mask_ref[t] == 1)(lambda: step(True))
    pl.when(mask_ref[t] == 0)(lambda: step(False))

    @pl.when(last_ref[t] == 1)
    def _():
        lane = lax.broadcasted_iota(jnp.int32, (tq, LANES), 1)
        for pr in range(A_HEADS // 2):
            lo = acc_scr[2 * pr] / l_scr[2 * pr]
            hi = acc_scr[2 * pr + 1] / l_scr[2 * pr + 1]
            o_ref[:, pr * LANES:(pr + 1) * LANES] = jnp.where(lane < A_V, lo, hi).astype(BF16)


def _attn(q, k, v, q_off, tq, tk):
    b, sq, _ = q.shape
    sk = k.shape[1]
    nq, nk = sq // tq, sk // tk
    qi, ki, last, mask = [], [], [], []
    for i in range(nq):
        q_first, q_last = q_off + i * tq, q_off + i * tq + tq - 1
        j_end = min(nk - 1, (((q_last >> CHUNK_SHIFT) + 1) * CHUNK - 1) // tk)
        for j in range(j_end + 1):
            qi.append(i)
            ki.append(j)
            last.append(int(j == j_end))
            mask.append(int(((j * tk + tk - 1) >> CHUNK_SHIFT) > (q_first >> CHUNK_SHIFT)))
    arrs = [jnp.asarray(np.asarray(a, np.int32)) for a in (qi, ki, last, mask)]
    grid_spec = pltpu.PrefetchScalarGridSpec(
        num_scalar_prefetch=4,
        grid=(b, len(qi)),
        in_specs=[pl.BlockSpec((None, tq, A_KPAD), lambda bi, t, qi, ki, la, ma: (bi, qi[t], 0)),
                  pl.BlockSpec((None, tk, A_KPAD), lambda bi, t, qi, ki, la, ma: (bi, ki[t], 0)),
                  pl.BlockSpec((None, tk, A_WIDTH), lambda bi, t, qi, ki, la, ma: (bi, ki[t], 0))],
        out_specs=pl.BlockSpec((None, tq, A_WIDTH), lambda bi, t, qi, ki, la, ma: (bi, qi[t], 0)),
        scratch_shapes=[pltpu.VMEM((A_HEADS, tq, 1), F32), pltpu.VMEM((A_HEADS, tq, 1), F32),
                        pltpu.VMEM((A_HEADS, tq, LANES), F32)],
    )
    return pl.pallas_call(
        functools.partial(_attn_kernel, q_off=q_off),
        grid_spec=grid_spec,
        out_shape=jax.ShapeDtypeStruct((b, sq, A_WIDTH), BF16),
        compiler_params=_params(("parallel", "arbitrary")),
        name="attn",
    )(*arrs, q, k, v)


def _outproj_kernel(x_ref, hm_ref, at_ref, w_out, post_g, o_ref):
    merged = jnp.concatenate([hm_ref[...], at_ref[...]], axis=-1)
    y = jnp.dot(merged, w_out[...], preferred_element_type=F32)
    o_ref[...] = x_ref[...] + _rms(y, post_g[...])


def _outproj(x, hm, at, w_out, post_g):
    n = x.shape[0]
    tm = _row_tile(n, 512)
    tok = lambda w: pl.BlockSpec((tm, w), lambda i: (i, 0))
    return pl.pallas_call(
        _outproj_kernel,
        grid=(n // tm,),
        in_specs=[tok(D_MODEL), tok(M_WIDTH), tok(A_WIDTH),
                  _const_spec((M_WIDTH + A_WIDTH, D_MODEL)), _const_spec((1, D_MODEL))],
        out_specs=tok(D_MODEL),
        out_shape=jax.ShapeDtypeStruct((n, D_MODEL), F32),
        compiler_params=_params(("parallel",)),
        name="outproj",
    )(x, hm, at, w_out, post_g)


def _rope_tables(pos):
    half = A_ROPE // 2
    freq = ROPE_THETA ** (-jnp.arange(half, dtype=F32) / half)
    ang = pos.astype(F32)[:, None] * freq[None, :]
    cos, sin = jnp.cos(ang), jnp.sin(ang)
    n = pos.shape[0]
    ones = jnp.ones((n, A_NOPE), F32)
    z = lambda w: jnp.zeros((n, w), F32)
    tail = LANES - A_NOPE - A_ROPE
    cos_t = jnp.concatenate([ones, cos, cos, z(tail)], axis=1)
    sin_up = jnp.concatenate([z(A_NOPE + half), sin, z(tail)], axis=1)
    sin_dn = jnp.concatenate([z(A_NOPE), -sin, z(half + tail)], axis=1)
    return cos_t, sin_up, sin_dn


def _pad_heads(w, widths, n_heads):
    kdim = w.shape[0]
    w = w.reshape(kdim, n_heads, sum(widths))
    w = jnp.pad(w, ((0, 0), (0, 0), (0, LANES - sum(widths))))
    return w.reshape(kdim, n_heads * LANES)


def _layer(x, pos, ff1, mix, ff2, w_ukv_k, w_ukv_v, cache, mlstm_block, attn_tiles):
    b, s, _ = x.shape
    mix_pre_g, w_in_p, bias_col, bias_row, q_a_g, w_uq_p, kv_a_g, head_g, w_out, mix_post_g = mix
    x1 = _ffn(x.reshape(b * s, D_MODEL), *ff1).reshape(b, s, D_MODEL)
    mq, mk, mv, mo, aux, grow, q, c_kv = _proj(x1, _rope_tables(pos), mix_pre_g, w_in_p,
                                               q_a_g, w_uq_p, kv_a_g)
    k_rope = aux[:, :, AUX_ROPE_LANE:AUX_ROPE_LANE + A_ROPE]
    if cache is None:
        c_all, aux_all, q_off = c_kv, aux, 0
        c0 = jnp.zeros((b, M_HEADS, M_HEAD_DIM, M_HEAD_DIM), F32)
        n0 = jnp.zeros((b, M_HEADS, 1, M_HEAD_DIM), F32)
        m0 = jnp.zeros((b, M_HEADS, 1, LANES), F32)
    else:
        cache_kv, cache_kr, c0, n0, m0 = cache
        q_off = cache_kv.shape[1]
        c_all = jnp.concatenate([cache_kv, c_kv], axis=1)
        kr_pad = jnp.pad(cache_kr, ((0, 0), (0, 0), (AUX_ROPE_LANE, LANES - AUX_ROPE_LANE - A_ROPE)))
        aux_all = jnp.concatenate([kr_pad, aux], axis=1)
        n0 = n0.reshape(b, M_HEADS, 1, M_HEAD_DIM)
        m0 = jnp.broadcast_to(m0[:, :, None, None], (b, M_HEADS, 1, LANES))
    k, v = _expand(c_all, aux_all, w_ukv_k, w_ukv_v)
    hm, c_new, n_new, m_new = _mlstm(mq, mk, mv, mo, aux, grow, bias_col, bias_row, head_g,
                                     c0, n0, m0, mlstm_block)
    tq, tk = attn_tiles
    at = _attn(q, k, v, q_off, tq, min(tk, k.shape[1]))
    n_tok = b * s
    x2 = _outproj(x1.reshape(n_tok, D_MODEL), hm.reshape(n_tok, M_WIDTH),
                  at.reshape(n_tok, A_WIDTH), w_out, mix_post_g)
    y = _ffn(x2, *ff2).reshape(b, s, D_MODEL)
    state = (c_kv, k_rope, c_new, n_new.reshape(b, M_HEADS, M_HEAD_DIM), m_new[:, :, 0, 0])
    return y, state


def kernel(x_prompt, x_sample, cache_kv_latent, cache_k_rope, state_C, state_n, state_m, ff1_pre_g, ff1_w_gate, ff1_w_up, ff1_w_down, ff1_post_g, mix_pre_g, w_in, b_igate, b_fgate, q_a_g, w_uq, kv_a_g, w_ukv, m_head_g, w_out, mix_post_g, ff2_pre_g, ff2_w_gate, ff2_w_up, ff2_w_down, ff2_post_g):
    depth = w_in.shape[0]
    y_p, y_s = x_prompt, x_sample
    pos_p = jnp.arange(x_prompt.shape[1])
    pos_s = cache_kv_latent.shape[2] + jnp.arange(x_sample.shape[1])
    new_p, new_s = [], []
    row = lambda g: g.reshape(1, -1)
    for l in range(depth):
        ff1 = (row(ff1_pre_g[l]), ff1_w_gate[l].astype(BF16), ff1_w_up[l].astype(BF16),
               ff1_w_down[l].astype(BF16), row(ff1_post_g[l]))
        ff2 = (row(ff2_pre_g[l]), ff2_w_gate[l].astype(BF16), ff2_w_up[l].astype(BF16),
               ff2_w_down[l].astype(BF16), row(ff2_post_g[l]))
        wl = w_in[l]
        n_m = 4 * M_WIDTH
        gates = wl[:, n_m:n_m + 2 * M_HEADS]
        aq_akv = wl[:, n_m + 2 * M_HEADS:n_m + 2 * M_HEADS + Q_LORA + KV_LORA]
        ar = wl[:, n_m + 2 * M_HEADS + Q_LORA + KV_LORA:]
        zc = lambda w: jnp.zeros((D_MODEL, w), F32)
        aux_cols = jnp.concatenate([gates, zc(AUX_ROPE_LANE - 2 * M_HEADS), ar,
                                    zc(LANES - AUX_ROPE_LANE - A_ROPE)], axis=1)
        w_in_p = jnp.concatenate([wl[:, :n_m], aq_akv, aux_cols], axis=1).astype(BF16)
        bias = jnp.concatenate([b_igate[l], b_fgate[l]])
        bias_col = jnp.pad(bias, (0, LANES - 2 * M_HEADS)).reshape(1, LANES)
        bias_row = bias.reshape(2 * M_HEADS, 1)
        w_uq_p = _pad_heads(w_uq[l], (A_NOPE, A_ROPE), A_HEADS).astype(BF16)
        wkv = w_ukv[l].reshape(KV_LORA, A_HEADS, A_NOPE + A_V)
        w_ukv_k = _pad_heads(wkv[:, :, :A_NOPE].reshape(KV_LORA, A_HEADS * A_NOPE), (A_NOPE,),
                             A_HEADS).astype(BF16)
        w_ukv_v = wkv[:, :, A_NOPE:].reshape(KV_LORA, A_WIDTH).astype(BF16)
        mix = (row(mix_pre_g[l]), w_in_p, bias_col, bias_row, row(q_a_g[l]), w_uq_p,
               row(kv_a_g[l]), row(m_head_g[l]), w_out[l].astype(BF16), row(mix_post_g[l]))
        y_p, st_p = _layer(y_p, pos_p, ff1, mix, ff2, w_ukv_k, w_ukv_v, None,
                           mlstm_block=256, attn_tiles=(512, 512))
        cache = (cache_kv_latent[l], cache_k_rope[l], state_C[l], state_n[l], state_m[l])
        y_s, st_s = _layer(y_s, pos_s, ff1, mix, ff2, w_ukv_k, w_ukv_v, cache,
                           mlstm_block=x_sample.shape[1], attn_tiles=(x_sample.shape[1], 4096))
        new_p.append(st_p)
        new_s.append(st_s)
    outs_p = [jnp.stack(a) for a in zip(*new_p)]
    outs_s = [jnp.stack(a) for a in zip(*new_s)]
    return (y_p, y_s, *outs_p, *outs_s)
```

```python
import functools
import math

import numpy as np
import jax
import jax.numpy as jnp
from jax import lax
from jax.experimental import pallas as pl
from jax.experimental.pallas import tpu as pltpu

F32 = jnp.float32
BF16 = jnp.bfloat16

D_MODEL = 1024
CHUNK = 64
CHUNK_SHIFT = 6
EPS = 1e-6
M_HEADS = 4
M_HEAD_DIM = 128
M_WIDTH = M_HEADS * M_HEAD_DIM
A_HEADS = 8
A_NOPE = 64
A_ROPE = 32
A_V = 64
A_WIDTH = A_HEADS * A_V
Q_LORA = 384
KV_LORA = 256
ROPE_THETA = 10000.0
A_SCALE = (A_NOPE + A_ROPE) ** -0.5
D_FF = 2816

LANES = 128
SUBLANES = 8
A_KPAD = A_HEADS * LANES
COL_MQ, COL_MK, COL_MO = 0, M_WIDTH, 2 * M_WIDTH
COL_AQ = 3 * M_WIDTH
COL_AKV = COL_AQ + Q_LORA
COL_AUX = COL_AKV + KV_LORA
IN_PAD = COL_AUX + LANES
AUX_ROPE_LANE = A_NOPE
VMEM_LIMIT = 56 * 1024 * 1024
NT_DIMS = (((1,), (1,)), ((), ()))


def _rms(x, g):
    return x * lax.rsqrt(jnp.mean(x * x, axis=-1, keepdims=True) + EPS) * g


def _const_spec(shape):
    nd = len(shape)
    return pl.BlockSpec(shape, lambda *_: (0,) * nd, pipeline_mode=pl.Buffered(1))


def _params(sem):
    return pltpu.CompilerParams(dimension_semantics=sem, vmem_limit_bytes=VMEM_LIMIT)


def _row_tile(n, cap, mult=16):
    best = n
    for t in range(mult, min(n, cap) + 1, mult):
        if n % t == 0:
            best = t
    return best


def _ffn_kernel(x_ref, pre_g, wg, wu, wd, post_g, o_ref):
    x = x_ref[...]
    h = _rms(x, pre_g[...]).astype(BF16)
    g = jnp.dot(h, wg[...], preferred_element_type=F32)
    u = jnp.dot(h, wu[...], preferred_element_type=F32)
    a = (g * jax.nn.sigmoid(g) * u).astype(BF16)
    y = jnp.dot(a, wd[...], preferred_element_type=F32)
    o_ref[...] = x + 0.5 * _rms(y, post_g[...])


def _ffn(x, pre_g, wg, wu, wd, post_g):
    n = x.shape[0]
    tm = _row_tile(n, 512)
    return pl.pallas_call(
        _ffn_kernel,
        grid=(n // tm,),
        in_specs=[pl.BlockSpec((tm, D_MODEL), lambda i: (i, 0)),
                  _const_spec((1, D_MODEL)), _const_spec((D_MODEL, D_FF)),
                  _const_spec((D_MODEL, D_FF)), _const_spec((D_FF, D_MODEL)),
                  _const_spec((1, D_MODEL))],
        out_specs=pl.BlockSpec((tm, D_MODEL), lambda i: (i, 0)),
        out_shape=jax.ShapeDtypeStruct((n, D_MODEL), F32),
        compiler_params=_params(("parallel",)),
        name="ffn",
    )(x, pre_g, wg, wu, wd, post_g)


def _rope_lanes(x, cos, sin_up, sin_dn):
    half = A_ROPE // 2
    return (x * cos + pltpu.roll(x, half, axis=1) * sin_up
            + pltpu.roll(x, LANES - half, axis=1) * sin_dn)


def _proj_kernel(x_ref, pre_g, w_in, w_vt, q_a_g, w_uq, kv_a_g, cos_ref, sup_ref, sdn_ref,
                 mq_ref, mk_ref, mvt_ref, mo_ref, aux_ref, grow_ref, q_ref, ckv_ref):
    h = _rms(x_ref[...], pre_g[...]).astype(BF16)
    p = jnp.dot(h, w_in[...], preferred_element_type=F32)
    mq_ref[...] = p[:, COL_MQ:COL_MQ + M_WIDTH].astype(BF16)
    mk_ref[...] = (p[:, COL_MK:COL_MK + M_WIDTH] * (M_HEAD_DIM ** -0.5)).astype(BF16)
    mo_ref[...] = p[:, COL_MO:COL_MO + M_WIDTH].astype(BF16)
    mvt_ref[...] = lax.dot_general(w_vt[...], h, NT_DIMS,
                                   preferred_element_type=F32).astype(BF16)
    cos, sup, sdn = cos_ref[...], sup_ref[...], sdn_ref[...]
    aux = _rope_lanes(p[:, COL_AUX:COL_AUX + LANES], cos, sup, sdn)
    aux_ref[...] = aux
    grow_ref[...] = aux.T[:2 * M_HEADS, :]
    ckv_ref[...] = _rms(p[:, COL_AKV:COL_AKV + KV_LORA], kv_a_g[...])
    qa = _rms(p[:, COL_AQ:COL_AQ + Q_LORA], q_a_g[...]).astype(BF16)
    q = jnp.dot(qa, w_uq[...], preferred_element_type=F32)
    q_scale = A_SCALE * math.log2(math.e)
    for hd in range(A_HEADS):
        sl = slice(hd * LANES, (hd + 1) * LANES)
        q_ref[:, sl] = (_rope_lanes(q[:, sl], cos, sup, sdn) * q_scale).astype(BF16)


def _proj(x, tables, pre_g, w_in, w_vt, q_a_g, w_uq, kv_a_g):
    b, s, _ = x.shape
    tm = _row_tile(s, 512, LANES)
    tok = lambda w: pl.BlockSpec((None, tm, w), lambda bi, j: (bi, j, 0))
    tok_t = lambda w: pl.BlockSpec((None, w, tm), lambda bi, j: (bi, 0, j))
    tab = pl.BlockSpec((tm, LANES), lambda bi, j: (j, 0))
    out_shape = [jax.ShapeDtypeStruct((b, s, M_WIDTH), BF16),
                 jax.ShapeDtypeStruct((b, s, M_WIDTH), BF16),
                 jax.ShapeDtypeStruct((b, M_WIDTH, s), BF16),
                 jax.ShapeDtypeStruct((b, s, M_WIDTH), BF16),
                 jax.ShapeDtypeStruct((b, s, LANES), F32),
                 jax.ShapeDtypeStruct((b, 2 * M_HEADS, s), F32),
                 jax.ShapeDtypeStruct((b, s, A_KPAD), BF16),
                 jax.ShapeDtypeStruct((b, s, KV_LORA), F32)]
    out_specs = [tok(M_WIDTH), tok(M_WIDTH), tok_t(M_WIDTH), tok(M_WIDTH), tok(LANES),
                 tok_t(2 * M_HEADS), tok(A_KPAD), tok(KV_LORA)]
    return pl.pallas_call(
        _proj_kernel,
        grid=(b, s // tm),
        in_specs=[tok(D_MODEL), _const_spec((1, D_MODEL)), _const_spec((D_MODEL, IN_PAD)),
                  _const_spec((M_WIDTH, D_MODEL)),
                  _const_spec((1, Q_LORA)), _const_spec((Q_LORA, A_KPAD)),
                  _const_spec((1, KV_LORA)), tab, tab, tab],
        out_specs=out_specs,
        out_shape=out_shape,
        compiler_params=_params(("parallel", "parallel")),
        name="proj",
    )(x, pre_g, w_in, w_vt, q_a_g, w_uq, kv_a_g, *tables)


def _expand_kernel(c_ref, aux_ref, wk, wv_t, k_ref, vt_ref):
    cb = c_ref[...].astype(BF16)
    aux = aux_ref[...]
    lane = lax.broadcasted_iota(jnp.int32, aux.shape, 1)
    k_rope = jnp.where(lane >= AUX_ROPE_LANE, aux, 0.0)
    kn = jnp.dot(cb, wk[...], preferred_element_type=F32)
    for hd in range(A_HEADS):
        sl = slice(hd * LANES, (hd + 1) * LANES)
        k_ref[:, sl] = (kn[:, sl] + k_rope).astype(BF16)
    vt_ref[...] = lax.dot_general(wv_t[...], cb, NT_DIMS,
                                  preferred_element_type=F32).astype(BF16)


def _expand(c, aux, wk, wv_t):
    b, t, _ = c.shape
    tm = _row_tile(t, 1024, LANES)
    tok = lambda w: pl.BlockSpec((None, tm, w), lambda bi, j: (bi, j, 0))
    return pl.pallas_call(
        _expand_kernel,
        grid=(b, t // tm),
        in_specs=[tok(KV_LORA), tok(LANES), _const_spec((KV_LORA, A_KPAD)),
                  _const_spec((A_WIDTH, KV_LORA))],
        out_specs=[tok(A_KPAD), pl.BlockSpec((None, A_WIDTH, tm), lambda bi, j: (bi, 0, j))],
        out_shape=[jax.ShapeDtypeStruct((b, t, A_KPAD), BF16),
                   jax.ShapeDtypeStruct((b, A_WIDTH, t), BF16)],
        compiler_params=_params(("parallel", "parallel")),
        name="expand",
    )(c, aux, wk, wv_t)


def _log_sigmoid(x):
    return jnp.minimum(x, 0.0) - jnp.log(1.0 + jnp.exp(-jnp.abs(x)))


def _lane_scan(x, op, fill):
    n = x.shape[1]
    idx = lax.broadcasted_iota(jnp.int32, x.shape, 1)
    k = 1
    while k < n:
        x = op(x, jnp.where(idx >= k, pltpu.roll(x, k, axis=1), fill))
        k *= 2
    return x


def _mlstm_kernel(mq_ref, mk_ref, mvt_ref, mo_ref, grow_ref, brow_ref, hg_ref,
                  c0_ref, n0_ref, m0_ref, hm_ref, c_ref, n_ref, m_ref):
    @pl.when(pl.program_id(1) == 0)
    def _():
        c_ref[...] = c0_ref[...]
        n_ref[...] = n0_ref[...]
        m_ref[...] = m0_ref[...]

    ln = mq_ref.shape[0]
    pre = grow_ref[...] + brow_ref[...]
    b_all = pltpu.roll(_lane_scan(_log_sigmoid(pre), jnp.add, 0.0), M_HEADS, axis=0)
    a_all = pre - b_all
    m_all = m_ref[...][:, :1]
    g_all = jnp.maximum(m_all, _lane_scan(a_all, jnp.maximum, -jnp.inf))
    a_cols = jnp.concatenate([a_all, jnp.zeros((LANES - SUBLANES, ln), F32)], axis=0).T
    src = lax.broadcasted_iota(jnp.int32, (ln, ln), 0)
    tgt = lax.broadcasted_iota(jnp.int32, (ln, ln), 1)
    causal = src <= tgt
    for hd in range(M_HEADS):
        sl = slice(hd * M_HEAD_DIM, (hd + 1) * M_HEAD_DIM)
        q, k, vt = mq_ref[:, sl], mk_ref[:, sl], mvt_ref[sl, :]
        m_prev = m_all[hd:hd + 1, :]
        c_prev = c_ref[hd]
        n_prev = n_ref[hd]
        a_row, b_row, g_row = a_all[hd:hd + 1, :], b_all[hd:hd + 1, :], g_all[hd:hd + 1, :]
        d = jnp.where(causal, jnp.exp(a_cols[:, hd:hd + 1] - g_row), 0.0)
        s = lax.dot_general(k, q, NT_DIMS, preferred_element_type=F32) * d
        w_carry = jnp.exp(m_prev - g_row)
        cq = lax.dot_general(c_prev.astype(BF16), q, NT_DIMS, preferred_element_type=F32)
        num = jnp.dot(vt, s.astype(BF16), preferred_element_type=F32) + w_carry * cq
        nq = lax.dot_general(n_prev.astype(BF16), q, NT_DIMS, preferred_element_type=F32)[:1, :]
        den = jnp.sum(s, axis=0, keepdims=True) + w_carry * nq
        den = jnp.maximum(jnp.abs(den), jnp.exp(-(b_row + g_row)))
        hh = num / den
        hh = hh * lax.rsqrt(jnp.mean(hh * hh, axis=0, keepdims=True) + EPS)
        gate = jax.nn.sigmoid(mo_ref[:, sl].astype(F32))
        hm_ref[:, sl] = (hh.T * hg_ref[:, sl] * gate).astype(BF16)
        g_last = g_row[:, ln - 1:ln]
        w_state = jnp.exp(m_prev - g_last)
        w_row = jnp.exp(a_row - g_last)
        vw = (vt.astype(F32) * w_row).astype(BF16)
        c_ref[hd] = w_state * c_prev + jnp.dot(vw, k, preferred_element_type=F32)
        w_rows = jnp.broadcast_to(w_row, (SUBLANES, ln)).astype(BF16)
        n_ref[hd] = w_state * n_prev + jnp.dot(w_rows, k, preferred_element_type=F32)
        m_ref[hd:hd + 1, :] = jnp.broadcast_to(b_row[:, ln - 1:ln] + g_last, (1, LANES))


def _mlstm(mq, mk, mvt, mo, grow, bias_row, head_g, c0, n0, m0, block):
    b, s, _ = mq.shape
    tok = lambda w: pl.BlockSpec((None, block, w), lambda bi, j: (bi, j, 0))
    tok_t = lambda w: pl.BlockSpec((None, w, block), lambda bi, j: (bi, 0, j))
    st_c = pl.BlockSpec((None, M_HEADS, M_HEAD_DIM, M_HEAD_DIM), lambda bi, j: (bi, 0, 0, 0))
    st_n = pl.BlockSpec((None, M_HEADS, SUBLANES, M_HEAD_DIM), lambda bi, j: (bi, 0, 0, 0))
    st_m = pl.BlockSpec((None, SUBLANES, LANES), lambda bi, j: (bi, 0, 0))
    return pl.pallas_call(
        _mlstm_kernel,
        grid=(b, s // block),
        in_specs=[tok(M_WIDTH), tok(M_WIDTH), tok_t(M_WIDTH), tok(M_WIDTH), tok_t(2 * M_HEADS),
                  _const_spec((2 * M_HEADS, 1)), _const_spec((1, M_WIDTH)), st_c, st_n, st_m],
        out_specs=[tok(M_WIDTH), st_c, st_n, st_m],
        out_shape=[jax.ShapeDtypeStruct((b, s, M_WIDTH), BF16),
                   jax.ShapeDtypeStruct((b, M_HEADS, M_HEAD_DIM, M_HEAD_DIM), F32),
                   jax.ShapeDtypeStruct((b, M_HEADS, SUBLANES, M_HEAD_DIM), F32),
                   jax.ShapeDtypeStruct((b, SUBLANES, LANES), F32)],
        compiler_params=_params(("parallel", "arbitrary")),
        name="mlstm",
    )(mq, mk, mvt, mo, grow, bias_row, head_g, c0, n0, m0)


def _attn_kernel(qi_ref, ki_ref, last_ref, mask_ref, q_ref, k_ref, vt_ref, o_ref,
                 m_scr, l_scr, acc_scr, *, q_off):
    t = pl.program_id(1)
    qi, ki = qi_ref[t], ki_ref[t]
    tq, tk = q_ref.shape[0], k_ref.shape[0]

    @pl.when(ki == 0)
    def _():
        m_scr[...] = jnp.full(m_scr.shape, -jnp.inf, F32)
        l_scr[...] = jnp.zeros(l_scr.shape, F32)
        acc_scr[...] = jnp.zeros(acc_scr.shape, F32)

    def step(masked):
        if masked:
            k_chunk = (ki * tk + lax.broadcasted_iota(jnp.int32, (tk, tq), 0)) >> CHUNK_SHIFT
            q_chunk = (q_off + qi * tq + lax.broadcasted_iota(jnp.int32, (tk, tq), 1)) >> CHUNK_SHIFT
            visible = q_chunk >= k_chunk
        for hd in range(A_HEADS):
            sl = slice(hd * LANES, (hd + 1) * LANES)
            s = lax.dot_general(k_ref[:, sl], q_ref[:, sl], NT_DIMS, preferred_element_type=F32)
            if masked:
                s = jnp.where(visible, s, -jnp.inf)
            m_prev = m_scr[hd]
            m_new = jnp.maximum(m_prev, jnp.max(s, axis=0, keepdims=True))
            alpha = jnp.exp2(m_prev - m_new)
            p = jnp.exp2(s - m_new)
            l_scr[hd] = alpha * l_scr[hd] + jnp.sum(p, axis=0, keepdims=True)
            vt = vt_ref[hd * A_V:(hd + 1) * A_V, :]
            acc_scr[hd] = alpha * acc_scr[hd] + jnp.dot(vt, p.astype(BF16),
                                                        preferred_element_type=F32)
            m_scr[hd] = m_new

    pl.when(mask_ref[t] == 1)(lambda: step(True))
    pl.when(mask_ref[t] == 0)(lambda: step(False))

    @pl.when(last_ref[t] == 1)
    def _():
        out_t = jnp.concatenate([acc_scr[hd] / l_scr[hd] for hd in range(A_HEADS)], axis=0)
        o_ref[...] = out_t.T.astype(BF16)


def _attn(q, k, vt, q_off, tq, tk):
    b, sq, _ = q.shape
    sk = k.shape[1]
    nq, nk = sq // tq, sk // tk
    qi, ki, last, mask = [], [], [], []
    for i in range(nq):
        q_first, q_last = q_off + i * tq, q_off + i * tq + tq - 1
        j_end = min(nk - 1, (((q_last >> CHUNK_SHIFT) + 1) * CHUNK - 1) // tk)
        for j in range(j_end + 1):
            qi.append(i)
            ki.append(j)
            last.append(int(j == j_end))
            mask.append(int(((j * tk + tk - 1) >> CHUNK_SHIFT) > (q_first >> CHUNK_SHIFT)))
    arrs = [jnp.asarray(np.asarray(a, np.int32)) for a in (qi, ki, last, mask)]
    grid_spec = pltpu.PrefetchScalarGridSpec(
        num_scalar_prefetch=4,
        grid=(b, len(qi)),
        in_specs=[pl.BlockSpec((None, tq, A_KPAD), lambda bi, t, qi, ki, la, ma: (bi, qi[t], 0)),
                  pl.BlockSpec((None, tk, A_KPAD), lambda bi, t, qi, ki, la, ma: (bi, ki[t], 0)),
                  pl.BlockSpec((None, A_WIDTH, tk), lambda bi, t, qi, ki, la, ma: (bi, 0, ki[t]))],
        out_specs=pl.BlockSpec((None, tq, A_WIDTH), lambda bi, t, qi, ki, la, ma: (bi, qi[t], 0)),
        scratch_shapes=[pltpu.VMEM((A_HEADS, 1, tq), F32), pltpu.VMEM((A_HEADS, 1, tq), F32),
                        pltpu.VMEM((A_HEADS, A_V, tq), F32)],
    )
    return pl.pallas_call(
        functools.partial(_attn_kernel, q_off=q_off),
        grid_spec=grid_spec,
        out_shape=jax.ShapeDtypeStruct((b, sq, A_WIDTH), BF16),
        compiler_params=_params(("parallel", "arbitrary")),
        name="attn",
    )(*arrs, q, k, vt)


def _outproj_kernel(x_ref, hm_ref, at_ref, w_out, post_g, o_ref):
    merged = jnp.concatenate([hm_ref[...], at_ref[...]], axis=-1)
    y = jnp.dot(merged, w_out[...], preferred_element_type=F32)
    o_ref[...] = x_ref[...] + _rms(y, post_g[...])


def _outproj(x, hm, at, w_out, post_g):
    n = x.shape[0]
    tm = _row_tile(n, 512)
    tok = lambda w: pl.BlockSpec((tm, w), lambda i: (i, 0))
    return pl.pallas_call(
        _outproj_kernel,
        grid=(n // tm,),
        in_specs=[tok(D_MODEL), tok(M_WIDTH), tok(A_WIDTH),
                  _const_spec((M_WIDTH + A_WIDTH, D_MODEL)), _const_spec((1, D_MODEL))],
        out_specs=tok(D_MODEL),
        out_shape=jax.ShapeDtypeStruct((n, D_MODEL), F32),
        compiler_params=_params(("parallel",)),
        name="outproj",
    )(x, hm, at, w_out, post_g)


def _rope_tables(pos):
    half = A_ROPE // 2
    freq = ROPE_THETA ** (-jnp.arange(half, dtype=F32) / half)
    ang = pos.astype(F32)[:, None] * freq[None, :]
    cos, sin = jnp.cos(ang), jnp.sin(ang)
    n = pos.shape[0]
    ones = jnp.ones((n, A_NOPE), F32)
    z = lambda w: jnp.zeros((n, w), F32)
    tail = LANES - A_NOPE - A_ROPE
    cos_t = jnp.concatenate([ones, cos, cos, z(tail)], axis=1)
    sin_up = jnp.concatenate([z(A_NOPE + half), sin, z(tail)], axis=1)
    sin_dn = jnp.concatenate([z(A_NOPE), -sin, z(half + tail)], axis=1)
    return cos_t, sin_up, sin_dn


def _pad_heads(w, widths, n_heads):
    kdim = w.shape[0]
    w = w.reshape(kdim, n_heads, sum(widths))
    w = jnp.pad(w, ((0, 0), (0, 0), (0, LANES - sum(widths))))
    return w.reshape(kdim, n_heads * LANES)


def _layer(x, pos, ff1, mix, ff2, w_ukv_k, w_ukv_vt, cache, mlstm_block, attn_tiles):
    b, s, _ = x.shape
    mix_pre_g, w_in_p, w_vt, bias_row, q_a_g, w_uq_p, kv_a_g, head_g, w_out, mix_post_g = mix
    x1 = _ffn(x.reshape(b * s, D_MODEL), *ff1).reshape(b, s, D_MODEL)
    mq, mk, mvt, mo, aux, grow, q, c_kv = _proj(x1, _rope_tables(pos), mix_pre_g, w_in_p, w_vt,
                                                q_a_g, w_uq_p, kv_a_g)
    k_rope = aux[:, :, AUX_ROPE_LANE:AUX_ROPE_LANE + A_ROPE]
    if cache is None:
        c_all, aux_all, q_off = c_kv, aux, 0
        c0 = jnp.zeros((b, M_HEADS, M_HEAD_DIM, M_HEAD_DIM), F32)
        n0 = jnp.zeros((b, M_HEADS, SUBLANES, M_HEAD_DIM), F32)
        m0 = jnp.zeros((b, SUBLANES, LANES), F32)
    else:
        cache_kv, cache_kr, c0, n0, m0 = cache
        q_off = cache_kv.shape[1]
        c_all = jnp.concatenate([cache_kv, c_kv], axis=1)
        kr_pad = jnp.pad(cache_kr, ((0, 0), (0, 0), (AUX_ROPE_LANE, LANES - AUX_ROPE_LANE - A_ROPE)))
        aux_all = jnp.concatenate([kr_pad, aux], axis=1)
        n0 = jnp.pad(n0[:, :, None, :], ((0, 0), (0, 0), (0, SUBLANES - 1), (0, 0)))
        m0 = jnp.broadcast_to(jnp.pad(m0, ((0, 0), (0, SUBLANES - M_HEADS)))[:, :, None],
                              (b, SUBLANES, LANES))
    k, vt = _expand(c_all, aux_all, w_ukv_k, w_ukv_vt)
    hm, c_new, n_new, m_new = _mlstm(mq, mk, mvt, mo, grow, bias_row, head_g, c0, n0, m0, mlstm_block)
    tq, tk = attn_tiles
    at = _attn(q, k, vt, q_off, tq, min(tk, k.shape[1]))
    n_tok = b * s
    x2 = _outproj(x1.reshape(n_tok, D_MODEL), hm.reshape(n_tok, M_WIDTH),
                  at.reshape(n_tok, A_WIDTH), w_out, mix_post_g)
    y = _ffn(x2, *ff2).reshape(b, s, D_MODEL)
    state = (c_kv, k_rope, c_new, n_new[:, :, 0, :], m_new[:, :M_HEADS, 0])
    return y, state


def kernel(x_prompt, x_sample, cache_kv_latent, cache_k_rope, state_C, state_n, state_m, ff1_pre_g, ff1_w_gate, ff1_w_up, ff1_w_down, ff1_post_g, mix_pre_g, w_in, b_igate, b_fgate, q_a_g, w_uq, kv_a_g, w_ukv, m_head_g, w_out, mix_post_g, ff2_pre_g, ff2_w_gate, ff2_w_up, ff2_w_down, ff2_post_g):
    depth = w_in.shape[0]
    y_p, y_s = x_prompt, x_sample
    pos_p = jnp.arange(x_prompt.shape[1])
    pos_s = cache_kv_latent.shape[2] + jnp.arange(x_sample.shape[1])
    new_p, new_s = [], []
    row = lambda g: g.reshape(1, -1)
    for l in range(depth):
        ff1 = (row(ff1_pre_g[l]), ff1_w_gate[l].astype(BF16), ff1_w_up[l].astype(BF16),
               ff1_w_down[l].astype(BF16), row(ff1_post_g[l]))
        ff2 = (row(ff2_pre_g[l]), ff2_w_gate[l].astype(BF16), ff2_w_up[l].astype(BF16),
               ff2_w_down[l].astype(BF16), row(ff2_post_g[l]))
        wl = w_in[l]
        n_m = 4 * M_WIDTH
        w_q, w_k, w_v, w_o = (wl[:, i * M_WIDTH:(i + 1) * M_WIDTH] for i in range(4))
        gates = wl[:, n_m:n_m + 2 * M_HEADS]
        aq_akv = wl[:, n_m + 2 * M_HEADS:n_m + 2 * M_HEADS + Q_LORA + KV_LORA]
        ar = wl[:, n_m + 2 * M_HEADS + Q_LORA + KV_LORA:]
        zc = lambda w: jnp.zeros((D_MODEL, w), F32)
        aux_cols = jnp.concatenate([gates, zc(AUX_ROPE_LANE - 2 * M_HEADS), ar,
                                    zc(LANES - AUX_ROPE_LANE - A_ROPE)], axis=1)
        w_in_p = jnp.concatenate([w_q, w_k, w_o, aq_akv, aux_cols], axis=1).astype(BF16)
        w_vt = w_v.T.astype(BF16)
        bias_row = jnp.concatenate([b_igate[l], b_fgate[l]]).reshape(2 * M_HEADS, 1)
        w_uq_p = _pad_heads(w_uq[l], (A_NOPE, A_ROPE), A_HEADS).astype(BF16)
        wkv = w_ukv[l].reshape(KV_LORA, A_HEADS, A_NOPE + A_V)
        w_ukv_k = _pad_heads(wkv[:, :, :A_NOPE].reshape(KV_LORA, A_HEADS * A_NOPE), (A_NOPE,),
                             A_HEADS).astype(BF16)
        w_ukv_vt = wkv[:, :, A_NOPE:].reshape(KV_LORA, A_WIDTH).T.astype(BF16)
        mix = (row(mix_pre_g[l]), w_in_p, w_vt, bias_row, row(q_a_g[l]), w_uq_p,
               row(kv_a_g[l]), row(m_head_g[l]), w_out[l].astype(BF16), row(mix_post_g[l]))
        y_p, st_p = _layer(y_p, pos_p, ff1, mix, ff2, w_ukv_k, w_ukv_vt, None,
                           mlstm_block=256, attn_tiles=(512, 512))
        cache = (cache_kv_latent[l], cache_k_rope[l], state_C[l], state_n[l], state_m[l])
        y_s, st_s = _layer(y_s, pos_s, ff1, mix, ff2, w_ukv_k, w_ukv_vt, cache,
                           mlstm_block=x_sample.shape[1], attn_tiles=(x_sample.shape[1], 4096))
        new_p.append(st_p)
        new_s.append(st_s)
    outs_p = [jnp.stack(a) for a in zip(*new_p)]
    outs_s = [jnp.stack(a) for a in zip(*new_s)]
    return (y_p, y_s, *outs_p, *outs_s)
```

```python
import functools
import math

import numpy as np
import jax
import jax.numpy as jnp
from jax import lax
from jax.experimental import pallas as pl
from jax.experimental.pallas import tpu as pltpu

F32 = jnp.float32
BF16 = jnp.bfloat16

D_MODEL = 1024
CHUNK = 64
CHUNK_SHIFT = 6
EPS = 1e-6
M_HEADS = 4
M_HEAD_DIM = 128
M_WIDTH = M_HEADS * M_HEAD_DIM
A_HEADS = 8
A_NOPE = 64
A_ROPE = 32
A_V = 64
A_WIDTH = A_HEADS * A_V
Q_LORA = 384
KV_LORA = 256
ROPE_THETA = 10000.0
A_SCALE = (A_NOPE + A_ROPE) ** -0.5
D_FF = 2816

LANES = 128
SUBLANES = 8
A_KPAD = A_HEADS * LANES
A_ONES = 16
COL_MQ, COL_MK, COL_MO = 0, M_WIDTH, 2 * M_WIDTH
COL_AQ = 3 * M_WIDTH
COL_AKV = COL_AQ + Q_LORA
COL_AUX = COL_AKV + KV_LORA
IN_PAD = COL_AUX + LANES
AUX_ROPE_LANE = A_NOPE
GST_A, GST_B, GST_CMAX, GST_ROWS = 0, SUBLANES, 2 * SUBLANES, 3 * SUBLANES
VMEM_LIMIT = 56 * 1024 * 1024
NT_DIMS = (((1,), (1,)), ((), ()))


def _rms(x, g):
    return x * lax.rsqrt(jnp.mean(x * x, axis=-1, keepdims=True) + EPS) * g


def _const_spec(shape):
    nd = len(shape)
    return pl.BlockSpec(shape, lambda *_: (0,) * nd, pipeline_mode=pl.Buffered(1))


def _params(sem):
    return pltpu.CompilerParams(dimension_semantics=sem, vmem_limit_bytes=VMEM_LIMIT)


def _row_tile(n, cap, mult=16):
    best = n
    for t in range(mult, min(n, cap) + 1, mult):
        if n % t == 0:
            best = t
    return best


def _ffn_kernel(x_ref, pre_g, wg, wu, wd, post_g, o_ref):
    x = x_ref[...]
    h = _rms(x, pre_g[...]).astype(BF16)
    g = jnp.dot(h, wg[...], preferred_element_type=F32)
    u = jnp.dot(h, wu[...], preferred_element_type=F32)
    a = (g * jax.nn.sigmoid(g) * u).astype(BF16)
    y = jnp.dot(a, wd[...], preferred_element_type=F32)
    o_ref[...] = x + 0.5 * _rms(y, post_g[...])


def _ffn(x, pre_g, wg, wu, wd, post_g):
    n = x.shape[0]
    tm = _row_tile(n, 512)
    return pl.pallas_call(
        _ffn_kernel,
        grid=(n // tm,),
        in_specs=[pl.BlockSpec((tm, D_MODEL), lambda i: (i, 0)),
                  _const_spec((1, D_MODEL)), _const_spec((D_MODEL, D_FF)),
                  _const_spec((D_MODEL, D_FF)), _const_spec((D_FF, D_MODEL)),
                  _const_spec((1, D_MODEL))],
        out_specs=pl.BlockSpec((tm, D_MODEL), lambda i: (i, 0)),
        out_shape=jax.ShapeDtypeStruct((n, D_MODEL), F32),
        compiler_params=_params(("parallel",)),
        name="ffn",
    )(x, pre_g, wg, wu, wd, post_g)


def _log_sigmoid(x):
    return jnp.minimum(x, 0.0) - jnp.log(1.0 + jnp.exp(-jnp.abs(x)))


def _lane_scan(x, op, fill):
    n = x.shape[1]
    idx = lax.broadcasted_iota(jnp.int32, x.shape, 1)
    k = 1
    while k < n:
        x = op(x, jnp.where(idx >= k, pltpu.roll(x, k, axis=1), fill))
        k *= 2
    return x


def _rope_lanes(x, cos, sin_up, sin_dn):
    half = A_ROPE // 2
    return (x * cos + pltpu.roll(x, half, axis=1) * sin_up
            + pltpu.roll(x, LANES - half, axis=1) * sin_dn)


def _proj_kernel(x_ref, pre_g, w_in, w_vt, brow_ref, q_a_g, w_uq, kv_a_g, cos_ref, sup_ref, sdn_ref,
                 mq_ref, mk_ref, mvt_ref, mo_ref, aux_ref, gst_ref, q_ref, ckv_ref, *, block):
    h = _rms(x_ref[...], pre_g[...]).astype(BF16)
    p = jnp.dot(h, w_in[...], preferred_element_type=F32)
    mq_ref[...] = p[:, COL_MQ:COL_MQ + M_WIDTH].astype(BF16)
    mk_ref[...] = (p[:, COL_MK:COL_MK + M_WIDTH] * (M_HEAD_DIM ** -0.5)).astype(BF16)
    mo_ref[...] = p[:, COL_MO:COL_MO + M_WIDTH].astype(BF16)
    mvt_ref[...] = lax.dot_general(w_vt[...], h, NT_DIMS,
                                   preferred_element_type=F32).astype(BF16)
    cos, sup, sdn = cos_ref[...], sup_ref[...], sdn_ref[...]
    aux = _rope_lanes(p[:, COL_AUX:COL_AUX + LANES], cos, sup, sdn)
    aux_ref[...] = aux
    pre = aux.T[:2 * M_HEADS, :] + brow_ref[...]
    for c in range(pre.shape[1] // block):
        cs = slice(c * block, (c + 1) * block)
        b_all = pltpu.roll(_lane_scan(_log_sigmoid(pre[:, cs]), jnp.add, 0.0), M_HEADS, axis=0)
        a_all = pre[:, cs] - b_all
        gst_ref[GST_A:GST_A + SUBLANES, cs] = a_all
        gst_ref[GST_B:GST_B + SUBLANES, cs] = b_all
        gst_ref[GST_CMAX:GST_CMAX + SUBLANES, cs] = _lane_scan(a_all, jnp.maximum, -jnp.inf)
    ckv_ref[...] = _rms(p[:, COL_AKV:COL_AKV + KV_LORA], kv_a_g[...])
    qa = _rms(p[:, COL_AQ:COL_AQ + Q_LORA], q_a_g[...]).astype(BF16)
    q = jnp.dot(qa, w_uq[...], preferred_element_type=F32)
    q_scale = A_SCALE * math.log2(math.e)
    for hd in range(A_HEADS):
        sl = slice(hd * LANES, (hd + 1) * LANES)
        q_ref[:, sl] = (_rope_lanes(q[:, sl], cos, sup, sdn) * q_scale).astype(BF16)


def _proj(x, tables, pre_g, w_in, w_vt, bias_row, q_a_g, w_uq, kv_a_g, block):
    b, s, _ = x.shape
    tm = _row_tile(s, 512, LANES)
    tok = lambda w: pl.BlockSpec((None, tm, w), lambda bi, j: (bi, j, 0))
    tok_t = lambda w: pl.BlockSpec((None, w, tm), lambda bi, j: (bi, 0, j))
    tab = pl.BlockSpec((tm, LANES), lambda bi, j: (j, 0))
    out_shape = [jax.ShapeDtypeStruct((b, s, M_WIDTH), BF16),
                 jax.ShapeDtypeStruct((b, s, M_WIDTH), BF16),
                 jax.ShapeDtypeStruct((b, M_WIDTH, s), BF16),
                 jax.ShapeDtypeStruct((b, s, M_WIDTH), BF16),
                 jax.ShapeDtypeStruct((b, s, LANES), F32),
                 jax.ShapeDtypeStruct((b, GST_ROWS, s), F32),
                 jax.ShapeDtypeStruct((b, s, A_KPAD), BF16),
                 jax.ShapeDtypeStruct((b, s, KV_LORA), F32)]
    out_specs = [tok(M_WIDTH), tok(M_WIDTH), tok_t(M_WIDTH), tok(M_WIDTH), tok(LANES),
                 tok_t(GST_ROWS), tok(A_KPAD), tok(KV_LORA)]
    return pl.pallas_call(
        functools.partial(_proj_kernel, block=block),
        grid=(b, s // tm),
        in_specs=[tok(D_MODEL), _const_spec((1, D_MODEL)), _const_spec((D_MODEL, IN_PAD)),
                  _const_spec((M_WIDTH, D_MODEL)), _const_spec((2 * M_HEADS, 1)),
                  _const_spec((1, Q_LORA)), _const_spec((Q_LORA, A_KPAD)),
                  _const_spec((1, KV_LORA)), tab, tab, tab],
        out_specs=out_specs,
        out_shape=out_shape,
        compiler_params=_params(("parallel", "parallel")),
        name="proj",
    )(x, pre_g, w_in, w_vt, bias_row, q_a_g, w_uq, kv_a_g, *tables)


def _expand_kernel(c_ref, aux_ref, wk, wv_t, k_ref, vt_ref):
    cb = c_ref[...].astype(BF16)
    aux = aux_ref[...]
    lane = lax.broadcasted_iota(jnp.int32, aux.shape, 1)
    k_rope = jnp.where(lane >= AUX_ROPE_LANE, aux, 0.0)
    kn = jnp.dot(cb, wk[...], preferred_element_type=F32)
    for hd in range(A_HEADS):
        sl = slice(hd * LANES, (hd + 1) * LANES)
        k_ref[:, sl] = (kn[:, sl] + k_rope).astype(BF16)
    vt_ref[...] = lax.dot_general(wv_t[...], cb, NT_DIMS,
                                  preferred_element_type=F32).astype(BF16)


def _expand(c, aux, wk, wv_t):
    b, t, _ = c.shape
    tm = _row_tile(t, 1024, LANES)
    tok = lambda w: pl.BlockSpec((None, tm, w), lambda bi, j: (bi, j, 0))
    return pl.pallas_call(
        _expand_kernel,
        grid=(b, t // tm),
        in_specs=[tok(KV_LORA), tok(LANES), _const_spec((KV_LORA, A_KPAD)),
                  _const_spec((A_WIDTH, KV_LORA))],
        out_specs=[tok(A_KPAD), pl.BlockSpec((None, A_WIDTH, tm), lambda bi, j: (bi, 0, j))],
        out_shape=[jax.ShapeDtypeStruct((b, t, A_KPAD), BF16),
                   jax.ShapeDtypeStruct((b, A_WIDTH, t), BF16)],
        compiler_params=_params(("parallel", "parallel")),
        name="expand",
    )(c, aux, wk, wv_t)


def _mlstm_kernel(mq_ref, mk_ref, mvt_ref, mo_ref, gst_ref, hg_ref,
                  c0_ref, n0_ref, m0_ref, hm_ref, c_ref, n_ref, m_ref):
    @pl.when(pl.program_id(1) == 0)
    def _():
        c_ref[...] = c0_ref[...]
        n_ref[...] = n0_ref[...]
        m_ref[...] = m0_ref[...]

    ln = mq_ref.shape[0]
    a_all = gst_ref[GST_A:GST_A + SUBLANES, :]
    b_all = gst_ref[GST_B:GST_B + SUBLANES, :]
    m_all = m_ref[...][:, :1]
    g_all = jnp.maximum(m_all, gst_ref[GST_CMAX:GST_CMAX + SUBLANES, :])
    a_cols = jnp.concatenate([a_all, jnp.zeros((LANES - SUBLANES, ln), F32)], axis=0).T
    src = lax.broadcasted_iota(jnp.int32, (ln, ln), 0)
    tgt = lax.broadcasted_iota(jnp.int32, (ln, ln), 1)
    causal = src <= tgt

    def outputs(hd):
        sl = slice(hd * M_HEAD_DIM, (hd + 1) * M_HEAD_DIM)
        q, k, vt = mq_ref[:, sl], mk_ref[:, sl], mvt_ref[sl, :]
        m_prev = m_all[hd:hd + 1, :]
        b_row, g_row = b_all[hd:hd + 1, :], g_all[hd:hd + 1, :]
        d = jnp.where(causal, jnp.exp(a_cols[:, hd:hd + 1] - g_row), 0.0)
        s = lax.dot_general(k, q, NT_DIMS, preferred_element_type=F32) * d
        w_carry = jnp.exp(m_prev - g_row)
        cq = lax.dot_general(c_ref[hd].astype(BF16), q, NT_DIMS, preferred_element_type=F32)
        num = jnp.dot(vt, s.astype(BF16), preferred_element_type=F32) + w_carry * cq
        nq = lax.dot_general(n_ref[hd].astype(BF16), q, NT_DIMS, preferred_element_type=F32)[:1, :]
        den = jnp.sum(s, axis=0, keepdims=True) + w_carry * nq
        den = jnp.maximum(jnp.abs(den), jnp.exp(-(b_row + g_row)))
        hh = num / den
        return hh * lax.rsqrt(jnp.mean(hh * hh, axis=0, keepdims=True) + EPS)

    def finish(hd, hh):
        sl = slice(hd * M_HEAD_DIM, (hd + 1) * M_HEAD_DIM)
        k, vt = mk_ref[:, sl], mvt_ref[sl, :]
        gate = jax.nn.sigmoid(mo_ref[:, sl].astype(F32))
        hm_ref[:, sl] = (hh.T * hg_ref[:, sl] * gate).astype(BF16)
        m_prev = m_all[hd:hd + 1, :]
        g_last = g_all[hd:hd + 1, ln - 1:ln]
        w_state = jnp.exp(m_prev - g_last)
        w_row = jnp.exp(a_all[hd:hd + 1, :] - g_last)
        vw = (vt.astype(F32) * w_row).astype(BF16)
        c_ref[hd] = w_state * c_ref[hd] + jnp.dot(vw, k, preferred_element_type=F32)
        w_rows = jnp.broadcast_to(w_row, (SUBLANES, ln)).astype(BF16)
        n_ref[hd] = w_state * n_ref[hd] + jnp.dot(w_rows, k, preferred_element_type=F32)
        m_ref[hd:hd + 1, :] = jnp.broadcast_to(b_all[hd:hd + 1, ln - 1:ln] + g_last, (1, LANES))

    hh_next = outputs(0)
    for hd in range(M_HEADS):
        hh, hh_next = hh_next, (outputs(hd + 1) if hd + 1 < M_HEADS else None)
        finish(hd, hh)


def _mlstm(mq, mk, mvt, mo, gst, head_g, c0, n0, m0, block):
    b, s, _ = mq.shape
    tok = lambda w: pl.BlockSpec((None, block, w), lambda bi, j: (bi, j, 0))
    tok_t = lambda w: pl.BlockSpec((None, w, block), lambda bi, j: (bi, 0, j))
    st_c = pl.BlockSpec((None, M_HEADS, M_HEAD_DIM, M_HEAD_DIM), lambda bi, j: (bi, 0, 0, 0))
    st_n = pl.BlockSpec((None, M_HEADS, SUBLANES, M_HEAD_DIM), lambda bi, j: (bi, 0, 0, 0))
    st_m = pl.BlockSpec((None, SUBLANES, LANES), lambda bi, j: (bi, 0, 0))
    return pl.pallas_call(
        _mlstm_kernel,
        grid=(b, s // block),
        in_specs=[tok(M_WIDTH), tok(M_WIDTH), tok_t(M_WIDTH), tok(M_WIDTH), tok_t(GST_ROWS),
                  _const_spec((1, M_WIDTH)), st_c, st_n, st_m],
        out_specs=[tok(M_WIDTH), st_c, st_n, st_m],
        out_shape=[jax.ShapeDtypeStruct((b, s, M_WIDTH), BF16),
                   jax.ShapeDtypeStruct((b, M_HEADS, M_HEAD_DIM, M_HEAD_DIM), F32),
                   jax.ShapeDtypeStruct((b, M_HEADS, SUBLANES, M_HEAD_DIM), F32),
                   jax.ShapeDtypeStruct((b, SUBLANES, LANES), F32)],
        compiler_params=_params(("parallel", "arbitrary")),
        name="mlstm",
    )(mq, mk, mvt, mo, gst, head_g, c0, n0, m0)


def _attn_kernel(qi_ref, ki_ref, last_ref, mask_ref, q_ref, k_ref, vt_ref, o_ref,
                 m_scr, acc_scr, *, q_off):
    t = pl.program_id(1)
    qi, ki = qi_ref[t], ki_ref[t]
    tq, tk = q_ref.shape[0], k_ref.shape[0]

    @pl.when(ki == 0)
    def _():
        m_scr[...] = jnp.full(m_scr.shape, -jnp.inf, F32)
        acc_scr[...] = jnp.zeros(acc_scr.shape, F32)

    def step(masked):
        if masked:
            k_chunk = (ki * tk + lax.broadcasted_iota(jnp.int32, (tk, tq), 0)) >> CHUNK_SHIFT
            q_chunk = (q_off + qi * tq + lax.broadcasted_iota(jnp.int32, (tk, tq), 1)) >> CHUNK_SHIFT
            visible = q_chunk >= k_chunk
        ones = jnp.ones((A_ONES, tk), BF16)

        def scores(hd):
            sl = slice(hd * LANES, (hd + 1) * LANES)
            s = lax.dot_general(k_ref[:, sl], q_ref[:, sl], NT_DIMS, preferred_element_type=F32)
            s = jnp.where(visible, s, -jnp.inf) if masked else s
            m_prev = m_scr[hd]
            m_new = jnp.maximum(m_prev, jnp.max(s, axis=0, keepdims=True))
            m_scr[hd] = m_new
            return s, m_new, jnp.exp2(m_prev - m_new)

        def probs(s, m_new):
            return jnp.exp2(s - m_new).astype(BF16)

        def accumulate(hd, p, alpha):
            vt = jnp.concatenate([vt_ref[hd * A_V:(hd + 1) * A_V, :], ones], axis=0)
            acc_scr[hd] = alpha * acc_scr[hd] + jnp.dot(vt, p, preferred_element_type=F32)

        st_s, st_p = {}, {}
        for i in range(A_HEADS + 2):
            if i < A_HEADS:
                st_s[i] = scores(i)
            if 0 <= i - 1 < A_HEADS:
                s, m_new, alpha = st_s.pop(i - 1)
                st_p[i - 1] = (probs(s, m_new), alpha)
            if 0 <= i - 2 < A_HEADS:
                accumulate(i - 2, *st_p.pop(i - 2))

    pl.when(mask_ref[t] == 1)(lambda: step(True))
    pl.when(mask_ref[t] == 0)(lambda: step(False))

    @pl.when(last_ref[t] == 1)
    def _():
        out_t = jnp.concatenate([acc_scr[hd][:A_V, :] / acc_scr[hd][A_V:A_V + 1, :]
                                 for hd in range(A_HEADS)], axis=0)
        o_ref[...] = out_t.T.astype(BF16)


def _attn(q, k, vt, q_off, tq, tk):
    b, sq, _ = q.shape
    sk = k.shape[1]
    nq, nk = sq // tq, sk // tk
    qi, ki, last, mask = [], [], [], []
    for i in range(nq):
        q_first, q_last = q_off + i * tq, q_off + i * tq + tq - 1
        j_end = min(nk - 1, (((q_last >> CHUNK_SHIFT) + 1) * CHUNK - 1) // tk)
        for j in range(j_end + 1):
            qi.append(i)
            ki.append(j)
            last.append(int(j == j_end))
            mask.append(int(((j * tk + tk - 1) >> CHUNK_SHIFT) > (q_first >> CHUNK_SHIFT)))
    arrs = [jnp.asarray(np.asarray(a, np.int32)) for a in (qi, ki, last, mask)]
    grid_spec = pltpu.PrefetchScalarGridSpec(
        num_scalar_prefetch=4,
        grid=(b, len(qi)),
        in_specs=[pl.BlockSpec((None, tq, A_KPAD), lambda bi, t, qi, ki, la, ma: (bi, qi[t], 0)),
                  pl.BlockSpec((None, tk, A_KPAD), lambda bi, t, qi, ki, la, ma: (bi, ki[t], 0)),
                  pl.BlockSpec((None, A_WIDTH, tk), lambda bi, t, qi, ki, la, ma: (bi, 0, ki[t]))],
        out_specs=pl.BlockSpec((None, tq, A_WIDTH), lambda bi, t, qi, ki, la, ma: (bi, qi[t], 0)),
        scratch_shapes=[pltpu.VMEM((A_HEADS, 1, tq), F32),
                        pltpu.VMEM((A_HEADS, A_V + A_ONES, tq), F32)],
    )
    return pl.pallas_call(
        functools.partial(_attn_kernel, q_off=q_off),
        grid_spec=grid_spec,
        out_shape=jax.ShapeDtypeStruct((b, sq, A_WIDTH), BF16),
        compiler_params=_params(("parallel", "arbitrary")),
        name="attn",
    )(*arrs, q, k, vt)


def _outproj_kernel(x_ref, hm_ref, at_ref, w_out, post_g, o_ref):
    merged = jnp.concatenate([hm_ref[...], at_ref[...]], axis=-1)
    y = jnp.dot(merged, w_out[...], preferred_element_type=F32)
    o_ref[...] = x_ref[...] + _rms(y, post_g[...])


def _outproj(x, hm, at, w_out, post_g):
    n = x.shape[0]
    tm = _row_tile(n, 512)
    tok = lambda w: pl.BlockSpec((tm, w), lambda i: (i, 0))
    return pl.pallas_call(
        _outproj_kernel,
        grid=(n // tm,),
        in_specs=[tok(D_MODEL), tok(M_WIDTH), tok(A_WIDTH),
                  _const_spec((M_WIDTH + A_WIDTH, D_MODEL)), _const_spec((1, D_MODEL))],
        out_specs=tok(D_MODEL),
        out_shape=jax.ShapeDtypeStruct((n, D_MODEL), F32),
        compiler_params=_params(("parallel",)),
        name="outproj",
    )(x, hm, at, w_out, post_g)


def _rope_tables(pos):
    half = A_ROPE // 2
    freq = ROPE_THETA ** (-jnp.arange(half, dtype=F32) / half)
    ang = pos.astype(F32)[:, None] * freq[None, :]
    cos, sin = jnp.cos(ang), jnp.sin(ang)
    n = pos.shape[0]
    ones = jnp.ones((n, A_NOPE), F32)
    z = lambda w: jnp.zeros((n, w), F32)
    tail = LANES - A_NOPE - A_ROPE
    cos_t = jnp.concatenate([ones, cos, cos, z(tail)], axis=1)
    sin_up = jnp.concatenate([z(A_NOPE + half), sin, z(tail)], axis=1)
    sin_dn = jnp.concatenate([z(A_NOPE), -sin, z(half + tail)], axis=1)
    return cos_t, sin_up, sin_dn


def _pad_heads(w, widths, n_heads):
    kdim = w.shape[0]
    w = w.reshape(kdim, n_heads, sum(widths))
    w = jnp.pad(w, ((0, 0), (0, 0), (0, LANES - sum(widths))))
    return w.reshape(kdim, n_heads * LANES)


def _layer(x, pos, ff1, mix, ff2, w_ukv_k, w_ukv_vt, cache, mlstm_block, attn_tiles):
    b, s, _ = x.shape
    mix_pre_g, w_in_p, w_vt, bias_row, q_a_g, w_uq_p, kv_a_g, head_g, w_out, mix_post_g = mix
    x1 = _ffn(x.reshape(b * s, D_MODEL), *ff1).reshape(b, s, D_MODEL)
    mq, mk, mvt, mo, aux, gst, q, c_kv = _proj(x1, _rope_tables(pos), mix_pre_g, w_in_p, w_vt,
                                               bias_row, q_a_g, w_uq_p, kv_a_g, mlstm_block)
    k_rope = aux[:, :, AUX_ROPE_LANE:AUX_ROPE_LANE + A_ROPE]
    if cache is None:
        c_all, aux_all, q_off = c_kv, aux, 0
        c0 = jnp.zeros((b, M_HEADS, M_HEAD_DIM, M_HEAD_DIM), F32)
        n0 = jnp.zeros((b, M_HEADS, SUBLANES, M_HEAD_DIM), F32)
        m0 = jnp.zeros((b, SUBLANES, LANES), F32)
    else:
        cache_kv, cache_kr, c0, n0, m0 = cache
        q_off = cache_kv.shape[1]
        c_all = jnp.concatenate([cache_kv, c_kv], axis=1)
        kr_pad = jnp.pad(cache_kr, ((0, 0), (0, 0), (AUX_ROPE_LANE, LANES - AUX_ROPE_LANE - A_ROPE)))
        aux_all = jnp.concatenate([kr_pad, aux], axis=1)
        n0 = jnp.pad(n0[:, :, None, :], ((0, 0), (0, 0), (0, SUBLANES - 1), (0, 0)))
        m0 = jnp.broadcast_to(jnp.pad(m0, ((0, 0), (0, SUBLANES - M_HEADS)))[:, :, None],
                              (b, SUBLANES, LANES))
    k, vt = _expand(c_all, aux_all, w_ukv_k, w_ukv_vt)
    hm, c_new, n_new, m_new = _mlstm(mq, mk, mvt, mo, gst, head_g, c0, n0, m0, mlstm_block)
    tq, tk = attn_tiles
    at = _attn(q, k, vt, q_off, tq, min(tk, k.shape[1]))
    n_tok = b * s
    x2 = _outproj(x1.reshape(n_tok, D_MODEL), hm.reshape(n_tok, M_WIDTH),
                  at.reshape(n_tok, A_WIDTH), w_out, mix_post_g)
    y = _ffn(x2, *ff2).reshape(b, s, D_MODEL)
    state = (c_kv, k_rope, c_new, n_new[:, :, 0, :], m_new[:, :M_HEADS, 0])
    return y, state


def kernel(x_prompt, x_sample, cache_kv_latent, cache_k_rope, state_C, state_n, state_m, ff1_pre_g, ff1_w_gate, ff1_w_up, ff1_w_down, ff1_post_g, mix_pre_g, w_in, b_igate, b_fgate, q_a_g, w_uq, kv_a_g, w_ukv, m_head_g, w_out, mix_post_g, ff2_pre_g, ff2_w_gate, ff2_w_up, ff2_w_down, ff2_post_g):
    depth = w_in.shape[0]
    y_p, y_s = x_prompt, x_sample
    pos_p = jnp.arange(x_prompt.shape[1])
    pos_s = cache_kv_latent.shape[2] + jnp.arange(x_sample.shape[1])
    new_p, new_s = [], []
    row = lambda g: g.reshape(1, -1)
    for l in range(depth):
        ff1 = (row(ff1_pre_g[l]), ff1_w_gate[l].astype(BF16), ff1_w_up[l].astype(BF16),
               ff1_w_down[l].astype(BF16), row(ff1_post_g[l]))
        ff2 = (row(ff2_pre_g[l]), ff2_w_gate[l].astype(BF16), ff2_w_up[l].astype(BF16),
               ff2_w_down[l].astype(BF16), row(ff2_post_g[l]))
        wl = w_in[l]
        n_m = 4 * M_WIDTH
        w_q, w_k, w_v, w_o = (wl[:, i * M_WIDTH:(i + 1) * M_WIDTH] for i in range(4))
        gates = wl[:, n_m:n_m + 2 * M_HEADS]
        aq_akv = wl[:, n_m + 2 * M_HEADS:n_m + 2 * M_HEADS + Q_LORA + KV_LORA]
        ar = wl[:, n_m + 2 * M_HEADS + Q_LORA + KV_LORA:]
        zc = lambda w: jnp.zeros((D_MODEL, w), F32)
        aux_cols = jnp.concatenate([gates, zc(AUX_ROPE_LANE - 2 * M_HEADS), ar,
                                    zc(LANES - AUX_ROPE_LANE - A_ROPE)], axis=1)
        w_in_p = jnp.concatenate([w_q, w_k, w_o, aq_akv, aux_cols], axis=1).astype(BF16)
        w_vt = w_v.T.astype(BF16)
        bias_row = jnp.concatenate([b_igate[l], b_fgate[l]]).reshape(2 * M_HEADS, 1)
        w_uq_p = _pad_heads(w_uq[l], (A_NOPE, A_ROPE), A_HEADS).astype(BF16)
        wkv = w_ukv[l].reshape(KV_LORA, A_HEADS, A_NOPE + A_V)
        w_ukv_k = _pad_heads(wkv[:, :, :A_NOPE].reshape(KV_LORA, A_HEADS * A_NOPE), (A_NOPE,),
                             A_HEADS).astype(BF16)
        w_ukv_vt = wkv[:, :, A_NOPE:].reshape(KV_LORA, A_WIDTH).T.astype(BF16)
        mix = (row(mix_pre_g[l]), w_in_p, w_vt, bias_row, row(q_a_g[l]), w_uq_p,
               row(kv_a_g[l]), row(m_head_g[l]), w_out[l].astype(BF16), row(mix_post_g[l]))
        y_p, st_p = _layer(y_p, pos_p, ff1, mix, ff2, w_ukv_k, w_ukv_vt, None,
                           mlstm_block=256, attn_tiles=(512, 512))
        cache = (cache_kv_latent[l], cache_k_rope[l], state_C[l], state_n[l], state_m[l])
        y_s, st_s = _layer(y_s, pos_s, ff1, mix, ff2, w_ukv_k, w_ukv_vt, cache,
                           mlstm_block=x_sample.shape[1], attn_tiles=(x_sample.shape[1], 4096))
        new_p.append(st_p)
        new_s.append(st_s)
    outs_p = [jnp.stack(a) for a in zip(*new_p)]
    outs_s = [jnp.stack(a) for a in zip(*new_s)]
    return (y_p, y_s, *outs_p, *outs_s)
```

```python
import functools
import math

import numpy as np
import jax
import jax.numpy as jnp
from jax import lax
from jax.experimental import pallas as pl
from jax.experimental.pallas import tpu as pltpu

F32 = jnp.float32
BF16 = jnp.bfloat16

D_MODEL = 1024
CHUNK = 64
CHUNK_SHIFT = 6
EPS = 1e-6
M_HEADS = 4
M_HEAD_DIM = 128
M_WIDTH = M_HEADS * M_HEAD_DIM
A_HEADS = 8
A_NOPE = 64
A_ROPE = 32
A_V = 64
A_WIDTH = A_HEADS * A_V
Q_LORA = 384
KV_LORA = 256
ROPE_THETA = 10000.0
A_SCALE = (A_NOPE + A_ROPE) ** -0.5
D_FF = 2816

LANES = 128
SUBLANES = 8
A_KPAD = A_HEADS * LANES
A_STRIP = 256
A_ONES = 16
COL_MQ, COL_MK, COL_MO = 0, M_WIDTH, 2 * M_WIDTH
COL_AQ = 3 * M_WIDTH
COL_AKV = COL_AQ + Q_LORA
COL_AUX = COL_AKV + KV_LORA
IN_PAD = COL_AUX + LANES
AUX_ROPE_LANE = A_NOPE
GST_A, GST_B, GST_CMAX, GST_ROWS = 0, SUBLANES, 2 * SUBLANES, 3 * SUBLANES
VMEM_LIMIT = 56 * 1024 * 1024
NT_DIMS = (((1,), (1,)), ((), ()))


def _rms(x, g):
    return x * lax.rsqrt(jnp.mean(x * x, axis=-1, keepdims=True) + EPS) * g


def _const_spec(shape):
    nd = len(shape)
    return pl.BlockSpec(shape, lambda *_: (0,) * nd, pipeline_mode=pl.Buffered(1))


def _params(sem):
    return pltpu.CompilerParams(dimension_semantics=sem, vmem_limit_bytes=VMEM_LIMIT)


def _row_tile(n, cap, mult=16):
    best = n
    for t in range(mult, min(n, cap) + 1, mult):
        if n % t == 0:
            best = t
    return best


def _ffn_kernel(x_ref, *refs, mixer):
    x = x_ref[...]
    if mixer:
        hm_ref, at_ref, w_out, mix_g, *refs = refs
        merged = jnp.concatenate([hm_ref[...], at_ref[...]], axis=-1)
        x = x + _rms(jnp.dot(merged, w_out[...], preferred_element_type=F32), mix_g[...])
    pre_g, wg, wu, wd, post_g, o_ref = refs
    h = _rms(x, pre_g[...]).astype(BF16)
    g = jnp.dot(h, wg[...], preferred_element_type=F32)
    u = jnp.dot(h, wu[...], preferred_element_type=F32)
    a = (g * jax.nn.sigmoid(g) * u).astype(BF16)
    y = jnp.dot(a, wd[...], preferred_element_type=F32)
    o_ref[...] = x + 0.5 * _rms(y, post_g[...])


def _ffn(x, ff, mixer=None):
    n = x.shape[0]
    tm = _row_tile(n, 512)
    tok = lambda w: pl.BlockSpec((tm, w), lambda i: (i, 0))
    mix_specs = [] if mixer is None else [tok(M_WIDTH), tok(A_WIDTH),
                                          _const_spec((M_WIDTH + A_WIDTH, D_MODEL)),
                                          _const_spec((1, D_MODEL))]
    return pl.pallas_call(
        functools.partial(_ffn_kernel, mixer=mixer is not None),
        grid=(n // tm,),
        in_specs=[tok(D_MODEL)] + mix_specs
        + [_const_spec((1, D_MODEL)), _const_spec((D_MODEL, D_FF)), _const_spec((D_MODEL, D_FF)),
           _const_spec((D_FF, D_MODEL)), _const_spec((1, D_MODEL))],
        out_specs=tok(D_MODEL),
        out_shape=jax.ShapeDtypeStruct((n, D_MODEL), F32),
        compiler_params=_params(("parallel",)),
        name="ffn",
    )(x, *(mixer or ()), *ff)


def _log_sigmoid(x):
    return jnp.minimum(x, 0.0) - jnp.log(1.0 + jnp.exp(-jnp.abs(x)))


def _lane_scan(x, op, fill):
    n = x.shape[1]
    idx = lax.broadcasted_iota(jnp.int32, x.shape, 1)
    k = 1
    while k < n:
        x = op(x, jnp.where(idx >= k, pltpu.roll(x, k, axis=1), fill))
        k *= 2
    return x


def _rope_lanes(x, cos, sin_up, sin_dn):
    half = A_ROPE // 2
    return (x * cos + pltpu.roll(x, half, axis=1) * sin_up
            + pltpu.roll(x, LANES - half, axis=1) * sin_dn)


def _proj_kernel(x_ref, pre_g, w_in, w_vt, brow_ref, q_a_g, w_uq, kv_a_g, cos_ref, sup_ref, sdn_ref,
                 mq_ref, mk_ref, mvt_ref, mo_ref, aux_ref, gst_ref, q_ref, ckv_ref, *, block):
    h = _rms(x_ref[...], pre_g[...]).astype(BF16)
    p = jnp.dot(h, w_in[...], preferred_element_type=F32)
    mq_ref[...] = p[:, COL_MQ:COL_MQ + M_WIDTH].astype(BF16)
    mk_ref[...] = (p[:, COL_MK:COL_MK + M_WIDTH] * (M_HEAD_DIM ** -0.5)).astype(BF16)
    mo_ref[...] = p[:, COL_MO:COL_MO + M_WIDTH].astype(BF16)
    mvt_ref[...] = lax.dot_general(w_vt[...], h, NT_DIMS,
                                   preferred_element_type=F32).astype(BF16)
    cos, sup, sdn = cos_ref[...], sup_ref[...], sdn_ref[...]
    aux = _rope_lanes(p[:, COL_AUX:COL_AUX + LANES], cos, sup, sdn)
    aux_ref[...] = aux
    pre = aux.T[:2 * M_HEADS, :] + brow_ref[...]
    for c in range(pre.shape[1] // block):
        cs = slice(c * block, (c + 1) * block)
        b_all = pltpu.roll(_lane_scan(_log_sigmoid(pre[:, cs]), jnp.add, 0.0), M_HEADS, axis=0)
        a_all = pre[:, cs] - b_all
        gst_ref[GST_A:GST_A + SUBLANES, cs] = a_all
        gst_ref[GST_B:GST_B + SUBLANES, cs] = b_all
        gst_ref[GST_CMAX:GST_CMAX + SUBLANES, cs] = _lane_scan(a_all, jnp.maximum, -jnp.inf)
    ckv_ref[...] = _rms(p[:, COL_AKV:COL_AKV + KV_LORA], kv_a_g[...])
    qa = _rms(p[:, COL_AQ:COL_AQ + Q_LORA], q_a_g[...]).astype(BF16)
    q = jnp.dot(qa, w_uq[...], preferred_element_type=F32)
    q_scale = A_SCALE * math.log2(math.e)
    for hd in range(A_HEADS):
        sl = slice(hd * LANES, (hd + 1) * LANES)
        q_ref[:, sl] = (_rope_lanes(q[:, sl], cos, sup, sdn) * q_scale).astype(BF16)


def _proj(x, tables, pre_g, w_in, w_vt, bias_row, q_a_g, w_uq, kv_a_g, block):
    b, s, _ = x.shape
    tm = _row_tile(s, 512, LANES)
    tok = lambda w: pl.BlockSpec((None, tm, w), lambda bi, j: (bi, j, 0))
    tok_t = lambda w: pl.BlockSpec((None, w, tm), lambda bi, j: (bi, 0, j))
    tab = pl.BlockSpec((tm, LANES), lambda bi, j: (j, 0))
    out_shape = [jax.ShapeDtypeStruct((b, s, M_WIDTH), BF16),
                 jax.ShapeDtypeStruct((b, s, M_WIDTH), BF16),
                 jax.ShapeDtypeStruct((b, M_WIDTH, s), BF16),
                 jax.ShapeDtypeStruct((b, s, M_WIDTH), BF16),
                 jax.ShapeDtypeStruct((b, s, LANES), F32),
                 jax.ShapeDtypeStruct((b, GST_ROWS, s), F32),
                 jax.ShapeDtypeStruct((b, s, A_KPAD), BF16),
                 jax.ShapeDtypeStruct((b, s, KV_LORA), F32)]
    out_specs = [tok(M_WIDTH), tok(M_WIDTH), tok_t(M_WIDTH), tok(M_WIDTH), tok(LANES),
                 tok_t(GST_ROWS), tok(A_KPAD), tok(KV_LORA)]
    return pl.pallas_call(
        functools.partial(_proj_kernel, block=block),
        grid=(b, s // tm),
        in_specs=[tok(D_MODEL), _const_spec((1, D_MODEL)), _const_spec((D_MODEL, IN_PAD)),
                  _const_spec((M_WIDTH, D_MODEL)), _const_spec((2 * M_HEADS, 1)),
                  _const_spec((1, Q_LORA)), _const_spec((Q_LORA, A_KPAD)),
                  _const_spec((1, KV_LORA)), tab, tab, tab],
        out_specs=out_specs,
        out_shape=out_shape,
        compiler_params=_params(("parallel", "parallel")),
        name="proj",
    )(x, pre_g, w_in, w_vt, bias_row, q_a_g, w_uq, kv_a_g, *tables)


def _expand_kernel(c_ref, aux_ref, wk, wv_t, k_ref, vt_ref):
    cb = c_ref[...].astype(BF16)
    aux = aux_ref[...]
    lane = lax.broadcasted_iota(jnp.int32, aux.shape, 1)
    k_rope = jnp.where(lane >= AUX_ROPE_LANE, aux, 0.0)
    kn = jnp.dot(cb, wk[...], preferred_element_type=F32)
    for hd in range(A_HEADS):
        sl = slice(hd * LANES, (hd + 1) * LANES)
        k_ref[:, sl] = (kn[:, sl] + k_rope).astype(BF16)
    vt_ref[...] = lax.dot_general(wv_t[...], cb, NT_DIMS,
                                  preferred_element_type=F32).astype(BF16)


def _expand(c, aux, wk, wv_t):
    b, t, _ = c.shape
    tm = _row_tile(t, 1024, LANES)
    tok = lambda w: pl.BlockSpec((None, tm, w), lambda bi, j: (bi, j, 0))
    return pl.pallas_call(
        _expand_kernel,
        grid=(b, t // tm),
        in_specs=[tok(KV_LORA), tok(LANES), _const_spec((KV_LORA, A_KPAD)),
                  _const_spec((A_WIDTH, KV_LORA))],
        out_specs=[tok(A_KPAD), pl.BlockSpec((None, A_WIDTH, tm), lambda bi, j: (bi, 0, j))],
        out_shape=[jax.ShapeDtypeStruct((b, t, A_KPAD), BF16),
                   jax.ShapeDtypeStruct((b, A_WIDTH, t), BF16)],
        compiler_params=_params(("parallel", "parallel")),
        name="expand",
    )(c, aux, wk, wv_t)


def _mlstm_kernel(mq_ref, mk_ref, mvt_ref, mo_ref, gst_ref, hg_ref,
                  c0_ref, n0_ref, m0_ref, hm_ref, c_ref, n_ref, m_ref):
    @pl.when(pl.program_id(1) == 0)
    def _():
        c_ref[...] = c0_ref[...]
        n_ref[...] = n0_ref[...]
        m_ref[...] = m0_ref[...]

    ln = mq_ref.shape[0]
    a_all = gst_ref[GST_A:GST_A + SUBLANES, :]
    b_all = gst_ref[GST_B:GST_B + SUBLANES, :]
    m_all = m_ref[...][:, :1]
    g_all = jnp.maximum(m_all, gst_ref[GST_CMAX:GST_CMAX + SUBLANES, :])
    a_cols = jnp.concatenate([a_all, jnp.zeros((LANES - SUBLANES, ln), F32)], axis=0).T
    src = lax.broadcasted_iota(jnp.int32, (ln, ln), 0)
    tgt = lax.broadcasted_iota(jnp.int32, (ln, ln), 1)
    causal = src <= tgt

    def outputs(hd):
        sl = slice(hd * M_HEAD_DIM, (hd + 1) * M_HEAD_DIM)
        q, k, vt = mq_ref[:, sl], mk_ref[:, sl], mvt_ref[sl, :]
        m_prev = m_all[hd:hd + 1, :]
        b_row, g_row = b_all[hd:hd + 1, :], g_all[hd:hd + 1, :]
        d = jnp.where(causal, jnp.exp(a_cols[:, hd:hd + 1] - g_row), 0.0)
        s = lax.dot_general(k, q, NT_DIMS, preferred_element_type=F32) * d
        w_carry = jnp.exp(m_prev - g_row)
        cq = lax.dot_general(c_ref[hd].astype(BF16), q, NT_DIMS, preferred_element_type=F32)
        num = jnp.dot(vt, s.astype(BF16), preferred_element_type=F32) + w_carry * cq
        nq = lax.dot_general(n_ref[hd].astype(BF16), q, NT_DIMS, preferred_element_type=F32)[:1, :]
        den = jnp.sum(s, axis=0, keepdims=True) + w_carry * nq
        den = jnp.maximum(jnp.abs(den), jnp.exp(-(b_row + g_row)))
        hh = num / den
        return hh * lax.rsqrt(jnp.mean(hh * hh, axis=0, keepdims=True) + EPS)

    def finish(hd, hh):
        sl = slice(hd * M_HEAD_DIM, (hd + 1) * M_HEAD_DIM)
        k, vt = mk_ref[:, sl], mvt_ref[sl, :]
        gate = jax.nn.sigmoid(mo_ref[:, sl].astype(F32))
        hm_ref[:, sl] = (hh.T * hg_ref[:, sl] * gate).astype(BF16)
        m_prev = m_all[hd:hd + 1, :]
        g_last = g_all[hd:hd + 1, ln - 1:ln]
        w_state = jnp.exp(m_prev - g_last)
        w_row = jnp.exp(a_all[hd:hd + 1, :] - g_last)
        vw = (vt.astype(F32) * w_row).astype(BF16)
        c_ref[hd] = w_state * c_ref[hd] + jnp.dot(vw, k, preferred_element_type=F32)
        w_rows = jnp.broadcast_to(w_row, (SUBLANES, ln)).astype(BF16)
        n_ref[hd] = w_state * n_ref[hd] + jnp.dot(w_rows, k, preferred_element_type=F32)
        m_ref[hd:hd + 1, :] = jnp.broadcast_to(b_all[hd:hd + 1, ln - 1:ln] + g_last, (1, LANES))

    hh_next = outputs(0)
    for hd in range(M_HEADS):
        hh, hh_next = hh_next, (outputs(hd + 1) if hd + 1 < M_HEADS else None)
        finish(hd, hh)


def _mlstm(mq, mk, mvt, mo, gst, head_g, c0, n0, m0, block):
    b, s, _ = mq.shape
    tok = lambda w: pl.BlockSpec((None, block, w), lambda bi, j: (bi, j, 0))
    tok_t = lambda w: pl.BlockSpec((None, w, block), lambda bi, j: (bi, 0, j))
    st_c = pl.BlockSpec((None, M_HEADS, M_HEAD_DIM, M_HEAD_DIM), lambda bi, j: (bi, 0, 0, 0))
    st_n = pl.BlockSpec((None, M_HEADS, SUBLANES, M_HEAD_DIM), lambda bi, j: (bi, 0, 0, 0))
    st_m = pl.BlockSpec((None, SUBLANES, LANES), lambda bi, j: (bi, 0, 0))
    return pl.pallas_call(
        _mlstm_kernel,
        grid=(b, s // block),
        in_specs=[tok(M_WIDTH), tok(M_WIDTH), tok_t(M_WIDTH), tok(M_WIDTH), tok_t(GST_ROWS),
                  _const_spec((1, M_WIDTH)), st_c, st_n, st_m],
        out_specs=[tok(M_WIDTH), st_c, st_n, st_m],
        out_shape=[jax.ShapeDtypeStruct((b, s, M_WIDTH), BF16),
                   jax.ShapeDtypeStruct((b, M_HEADS, M_HEAD_DIM, M_HEAD_DIM), F32),
                   jax.ShapeDtypeStruct((b, M_HEADS, SUBLANES, M_HEAD_DIM), F32),
                   jax.ShapeDtypeStruct((b, SUBLANES, LANES), F32)],
        compiler_params=_params(("parallel", "arbitrary")),
        name="mlstm",
    )(mq, mk, mvt, mo, gst, head_g, c0, n0, m0)


def _attn_kernel(qi_ref, ki_ref, last_ref, diag_ref, q_ref, k_ref, vt_ref, o_ref,
                 m_scr, acc_scr, *, has_diag):
    t = pl.program_id(1)
    ki = ki_ref[t]
    tq, tk = q_ref.shape[0], k_ref.shape[0]

    @pl.when(ki == 0)
    def _():
        m_scr[...] = jnp.full(m_scr.shape, -jnp.inf, F32)
        acc_scr[...] = jnp.zeros(acc_scr.shape, F32)

    def step(diag):
        strips = ([(r, r + A_STRIP, r) for r in range(0, tk, A_STRIP)] if diag else [(0, tk, 0)])
        if diag:
            k_chunk = lax.broadcasted_iota(jnp.int32, (A_STRIP, A_STRIP), 0) >> CHUNK_SHIFT
            q_chunk = lax.broadcasted_iota(jnp.int32, (A_STRIP, A_STRIP), 1) >> CHUNK_SHIFT
            visible = q_chunk >= k_chunk
        ones = jnp.ones((A_ONES, tk), BF16)

        def scores(hd):
            sl = slice(hd * LANES, (hd + 1) * LANES)
            m_prev = m_scr[hd]
            m_new, s_strips = m_prev, []
            for k_lo, k_hi, q_lo in strips:
                s = lax.dot_general(k_ref[k_lo:k_hi, sl], q_ref[q_lo:, sl], NT_DIMS,
                                    preferred_element_type=F32)
                if diag:
                    square = jnp.where(visible, s[:, :A_STRIP], -jnp.inf)
                    s = square if s.shape[1] == A_STRIP else jnp.concatenate(
                        [square, s[:, A_STRIP:]], axis=1)
                col_max = jnp.max(s, axis=0, keepdims=True)
                if q_lo:
                    col_max = jnp.concatenate([jnp.full((1, q_lo), -jnp.inf, F32), col_max], axis=1)
                m_new = jnp.maximum(m_new, col_max)
                s_strips.append(s)
            m_scr[hd] = m_new
            return s_strips, m_new, jnp.exp2(m_prev - m_new)

        def probs(s_strips, m_new):
            return [jnp.exp2(s - m_new[:, q_lo:]).astype(BF16)
                    for s, (_, _, q_lo) in zip(s_strips, strips)]

        def accumulate(hd, p_strips, alpha):
            vt = jnp.concatenate([vt_ref[hd * A_V:(hd + 1) * A_V, :], ones], axis=0)
            pv = [jnp.dot(vt[:, k_lo:k_hi], p, preferred_element_type=F32)
                  for p, (k_lo, k_hi, _) in zip(p_strips, strips)]
            if not diag:
                acc_scr[hd] = alpha * acc_scr[hd] + pv[0]
                return
            for c in range(0, tq, A_STRIP):
                cs = slice(c, c + A_STRIP)
                tot = sum(d[:, c - q_lo:c - q_lo + A_STRIP]
                          for d, (_, _, q_lo) in zip(pv, strips) if q_lo <= c)
                acc_scr[hd, :, cs] = alpha[:, cs] * acc_scr[hd, :, cs] + tot

        st_s, st_p = {}, {}
        for i in range(A_HEADS + 2):
            if i < A_HEADS:
                st_s[i] = scores(i)
            if 0 <= i - 1 < A_HEADS:
                s_strips, m_new, alpha = st_s.pop(i - 1)
                st_p[i - 1] = (probs(s_strips, m_new), alpha)
            if 0 <= i - 2 < A_HEADS:
                accumulate(i - 2, *st_p.pop(i - 2))

    if has_diag:
        pl.when(diag_ref[t] == 1)(lambda: step(True))
    pl.when(diag_ref[t] == 0)(lambda: step(False))

    @pl.when(last_ref[t] == 1)
    def _():
        out_t = jnp.concatenate([acc_scr[hd][:A_V, :] / acc_scr[hd][A_V:A_V + 1, :]
                                 for hd in range(A_HEADS)], axis=0)
        o_ref[...] = out_t.T.astype(BF16)


def _attn(q, k, vt, q_off, tq, tk):
    b, sq, _ = q.shape
    sk = k.shape[1]
    nq, nk = sq // tq, sk // tk
    qi, ki, last, diag = [], [], [], []
    for i in range(nq):
        q_first, q_last = q_off + i * tq, q_off + i * tq + tq - 1
        j_end = min(nk - 1, (((q_last >> CHUNK_SHIFT) + 1) * CHUNK - 1) // tk)
        for j in range(j_end + 1):
            needs_mask = ((j * tk + tk - 1) >> CHUNK_SHIFT) > (q_first >> CHUNK_SHIFT)
            if needs_mask and not (tq == tk and q_first == j * tk and tk % A_STRIP == 0):
                raise NotImplementedError("masked attention tiles must be diagonal and strip aligned")
            qi.append(i)
            ki.append(j)
            last.append(int(j == j_end))
            diag.append(int(needs_mask))
    arrs = [jnp.asarray(np.asarray(a, np.int32)) for a in (qi, ki, last, diag)]
    grid_spec = pltpu.PrefetchScalarGridSpec(
        num_scalar_prefetch=4,
        grid=(b, len(qi)),
        in_specs=[pl.BlockSpec((None, tq, A_KPAD), lambda bi, t, qi, ki, la, dg: (bi, qi[t], 0)),
                  pl.BlockSpec((None, tk, A_KPAD), lambda bi, t, qi, ki, la, dg: (bi, ki[t], 0)),
                  pl.BlockSpec((None, A_WIDTH, tk), lambda bi, t, qi, ki, la, dg: (bi, 0, ki[t]))],
        out_specs=pl.BlockSpec((None, tq, A_WIDTH), lambda bi, t, qi, ki, la, dg: (bi, qi[t], 0)),
        scratch_shapes=[pltpu.VMEM((A_HEADS, 1, tq), F32),
                        pltpu.VMEM((A_HEADS, A_V + A_ONES, tq), F32)],
    )
    return pl.pallas_call(
        functools.partial(_attn_kernel, has_diag=any(diag)),
        grid_spec=grid_spec,
        out_shape=jax.ShapeDtypeStruct((b, sq, A_WIDTH), BF16),
        compiler_params=_params(("parallel", "arbitrary")),
        name="attn",
    )(*arrs, q, k, vt)


def _rope_tables(pos):
    half = A_ROPE // 2
    freq = ROPE_THETA ** (-jnp.arange(half, dtype=F32) / half)
    ang = pos.astype(F32)[:, None] * freq[None, :]
    cos, sin = jnp.cos(ang), jnp.sin(ang)
    n = pos.shape[0]
    ones = jnp.ones((n, A_NOPE), F32)
    z = lambda w: jnp.zeros((n, w), F32)
    tail = LANES - A_NOPE - A_ROPE
    cos_t = jnp.concatenate([ones, cos, cos, z(tail)], axis=1)
    sin_up = jnp.concatenate([z(A_NOPE + half), sin, z(tail)], axis=1)
    sin_dn = jnp.concatenate([z(A_NOPE), -sin, z(half + tail)], axis=1)
    return cos_t, sin_up, sin_dn


def _pad_heads(w, widths, n_heads):
    kdim = w.shape[0]
    w = w.reshape(kdim, n_heads, sum(widths))
    w = jnp.pad(w, ((0, 0), (0, 0), (0, LANES - sum(widths))))
    return w.reshape(kdim, n_heads * LANES)


def _layer(x, pos, ff1, mix, ff2, w_ukv_k, w_ukv_vt, cache, mlstm_block, attn_tiles):
    b, s, _ = x.shape
    mix_pre_g, w_in_p, w_vt, bias_row, q_a_g, w_uq_p, kv_a_g, head_g, w_out, mix_post_g = mix
    x1 = _ffn(x.reshape(b * s, D_MODEL), ff1).reshape(b, s, D_MODEL)
    mq, mk, mvt, mo, aux, gst, q, c_kv = _proj(x1, _rope_tables(pos), mix_pre_g, w_in_p, w_vt,
                                               bias_row, q_a_g, w_uq_p, kv_a_g, mlstm_block)
    k_rope = aux[:, :, AUX_ROPE_LANE:AUX_ROPE_LANE + A_ROPE]
    if cache is None:
        c_all, aux_all, q_off = c_kv, aux, 0
        c0 = jnp.zeros((b, M_HEADS, M_HEAD_DIM, M_HEAD_DIM), F32)
        n0 = jnp.zeros((b, M_HEADS, SUBLANES, M_HEAD_DIM), F32)
        m0 = jnp.zeros((b, SUBLANES, LANES), F32)
    else:
        cache_kv, cache_kr, c0, n0, m0 = cache
        q_off = cache_kv.shape[1]
        c_all = jnp.concatenate([cache_kv, c_kv], axis=1)
        kr_pad = jnp.pad(cache_kr, ((0, 0), (0, 0), (AUX_ROPE_LANE, LANES - AUX_ROPE_LANE - A_ROPE)))
        aux_all = jnp.concatenate([kr_pad, aux], axis=1)
        n0 = jnp.pad(n0[:, :, None, :], ((0, 0), (0, 0), (0, SUBLANES - 1), (0, 0)))
        m0 = jnp.broadcast_to(jnp.pad(m0, ((0, 0), (0, SUBLANES - M_HEADS)))[:, :, None],
                              (b, SUBLANES, LANES))
    k, vt = _expand(c_all, aux_all, w_ukv_k, w_ukv_vt)
    hm, c_new, n_new, m_new = _mlstm(mq, mk, mvt, mo, gst, head_g, c0, n0, m0, mlstm_block)
    tq, tk = attn_tiles
    at = _attn(q, k, vt, q_off, tq, min(tk, k.shape[1]))
    n_tok = b * s
    mixer = (hm.reshape(n_tok, M_WIDTH), at.reshape(n_tok, A_WIDTH), w_out, mix_post_g)
    y = _ffn(x1.reshape(n_tok, D_MODEL), ff2, mixer).reshape(b, s, D_MODEL)
    state = (c_kv, k_rope, c_new, n_new[:, :, 0, :], m_new[:, :M_HEADS, 0])
    return y, state


def kernel(x_prompt, x_sample, cache_kv_latent, cache_k_rope, state_C, state_n, state_m, ff1_pre_g, ff1_w_gate, ff1_w_up, ff1_w_down, ff1_post_g, mix_pre_g, w_in, b_igate, b_fgate, q_a_g, w_uq, kv_a_g, w_ukv, m_head_g, w_out, mix_post_g, ff2_pre_g, ff2_w_gate, ff2_w_up, ff2_w_down, ff2_post_g):
    depth = w_in.shape[0]
    y_p, y_s = x_prompt, x_sample
    pos_p = jnp.arange(x_prompt.shape[1])
    pos_s = cache_kv_latent.shape[2] + jnp.arange(x_sample.shape[1])
    new_p, new_s = [], []
    row = lambda g: g.reshape(1, -1)
    for l in range(depth):
        ff1 = (row(ff1_pre_g[l]), ff1_w_gate[l].astype(BF16), ff1_w_up[l].astype(BF16),
               ff1_w_down[l].astype(BF16), row(ff1_post_g[l]))
        ff2 = (row(ff2_pre_g[l]), ff2_w_gate[l].astype(BF16), ff2_w_up[l].astype(BF16),
               ff2_w_down[l].astype(BF16), row(ff2_post_g[l]))
        wl = w_in[l]
        n_m = 4 * M_WIDTH
        w_q, w_k, w_v, w_o = (wl[:, i * M_WIDTH:(i + 1) * M_WIDTH] for i in range(4))
        gates = wl[:, n_m:n_m + 2 * M_HEADS]
        aq_akv = wl[:, n_m + 2 * M_HEADS:n_m + 2 * M_HEADS + Q_LORA + KV_LORA]
        ar = wl[:, n_m + 2 * M_HEADS + Q_LORA + KV_LORA:]
        zc = lambda w: jnp.zeros((D_MODEL, w), F32)
        aux_cols = jnp.concatenate([gates, zc(AUX_ROPE_LANE - 2 * M_HEADS), ar,
                                    zc(LANES - AUX_ROPE_LANE - A_ROPE)], axis=1)
        w_in_p = jnp.concatenate([w_q, w_k, w_o, aq_akv, aux_cols], axis=1).astype(BF16)
        w_vt = w_v.T.astype(BF16)
        bias_row = jnp.concatenate([b_igate[l], b_fgate[l]]).reshape(2 * M_HEADS, 1)
        w_uq_p = _pad_heads(w_uq[l], (A_NOPE, A_ROPE), A_HEADS).astype(BF16)
        wkv = w_ukv[l].reshape(KV_LORA, A_HEADS, A_NOPE + A_V)
        w_ukv_k = _pad_heads(wkv[:, :, :A_NOPE].reshape(KV_LORA, A_HEADS * A_NOPE), (A_NOPE,),
                             A_HEADS).astype(BF16)
        w_ukv_vt = wkv[:, :, A_NOPE:].reshape(KV_LORA, A_WIDTH).T.astype(BF16)
        mix = (row(mix_pre_g[l]), w_in_p, w_vt, bias_row, row(q_a_g[l]), w_uq_p,
               row(kv_a_g[l]), row(m_head_g[l]), w_out[l].astype(BF16), row(mix_post_g[l]))
        y_p, st_p = _layer(y_p, pos_p, ff1, mix, ff2, w_ukv_k, w_ukv_vt, None,
                           mlstm_block=256, attn_tiles=(1024, 1024))
        cache = (cache_kv_latent[l], cache_k_rope[l], state_C[l], state_n[l], state_m[l])
        y_s, st_s = _layer(y_s, pos_s, ff1, mix, ff2, w_ukv_k, w_ukv_vt, cache,
                           mlstm_block=x_sample.shape[1], attn_tiles=(x_sample.shape[1], 4096))
        new_p.append(st_p)
        new_s.append(st_s)
    outs_p = [jnp.stack(a) for a in zip(*new_p)]
    outs_s = [jnp.stack(a) for a in zip(*new_s)]
    return (y_p, y_s, *outs_p, *outs_s)
```

```python
import functools
import math

import numpy as np
import jax
import jax.numpy as jnp
from jax import lax
from jax.experimental import pallas as pl
from jax.experimental.pallas import tpu as pltpu

F32 = jnp.float32
BF16 = jnp.bfloat16

D_MODEL = 1024
CHUNK = 64
CHUNK_SHIFT = 6
EPS = 1e-6
M_HEADS = 4
M_HEAD_DIM = 128
M_WIDTH = M_HEADS * M_HEAD_DIM
A_HEADS = 8
A_NOPE = 64
A_ROPE = 32
A_V = 64
A_WIDTH = A_HEADS * A_V
Q_LORA = 384
KV_LORA = 256
ROPE_THETA = 10000.0
A_SCALE = (A_NOPE + A_ROPE) ** -0.5
D_FF = 2816

LANES = 128
SUBLANES = 8
A_KPAD = A_HEADS * LANES
A_PANEL = 512
A_PANEL_DIAG = 256
A_ONES = 16
COL_MQ, COL_MK, COL_MV, COL_MO = 0, M_WIDTH, 2 * M_WIDTH, 3 * M_WIDTH
COL_AQ = 4 * M_WIDTH
COL_AKV = COL_AQ + Q_LORA
COL_AUX = COL_AKV + KV_LORA
IN_PAD = COL_AUX + LANES
AUX_ROPE_LANE = A_NOPE
GST_A, GST_B, GST_CMAX, GST_ROWS = 0, SUBLANES, 2 * SUBLANES, 3 * SUBLANES
VMEM_LIMIT = 56 * 1024 * 1024
NT_DIMS = (((1,), (1,)), ((), ()))
TN_DIMS = (((0,), (0,)), ((), ()))


def _rms(x, g):
    return x * lax.rsqrt(jnp.mean(x * x, axis=-1, keepdims=True) + EPS) * g


def _const_spec(shape):
    nd = len(shape)
    return pl.BlockSpec(shape, lambda *_: (0,) * nd, pipeline_mode=pl.Buffered(1))


def _params(sem):
    return pltpu.CompilerParams(dimension_semantics=sem, vmem_limit_bytes=VMEM_LIMIT)


def _row_tile(n, cap, mult=16):
    best = n
    for t in range(mult, min(n, cap) + 1, mult):
        if n % t == 0:
            best = t
    return best


def _ffn_kernel(x_ref, *refs, mixer):
    x = x_ref[...]
    if mixer:
        hm_ref, at_ref, w_out, mix_g, *refs = refs
        merged = jnp.concatenate([hm_ref[...], at_ref[...]], axis=-1)
        x = x + _rms(jnp.dot(merged, w_out[...], preferred_element_type=F32), mix_g[...])
    pre_g, wg, wu, wd, post_g, o_ref = refs
    h = _rms(x, pre_g[...]).astype(BF16)
    g = jnp.dot(h, wg[...], preferred_element_type=F32)
    u = jnp.dot(h, wu[...], preferred_element_type=F32)
    a = (g * jax.nn.sigmoid(g) * u).astype(BF16)
    y = jnp.dot(a, wd[...], preferred_element_type=F32)
    o_ref[...] = x + 0.5 * _rms(y, post_g[...])


def _ffn(x, ff, mixer=None):
    n = x.shape[0]
    tm = _row_tile(n, 512)
    tok = lambda w: pl.BlockSpec((tm, w), lambda i: (i, 0))
    mix_specs = [] if mixer is None else [tok(M_WIDTH), tok(A_WIDTH),
                                          _const_spec((M_WIDTH + A_WIDTH, D_MODEL)),
                                          _const_spec((1, D_MODEL))]
    return pl.pallas_call(
        functools.partial(_ffn_kernel, mixer=mixer is not None),
        grid=(n // tm,),
        in_specs=[tok(D_MODEL)] + mix_specs
        + [_const_spec((1, D_MODEL)), _const_spec((D_MODEL, D_FF)), _const_spec((D_MODEL, D_FF)),
           _const_spec((D_FF, D_MODEL)), _const_spec((1, D_MODEL))],
        out_specs=tok(D_MODEL),
        out_shape=jax.ShapeDtypeStruct((n, D_MODEL), F32),
        compiler_params=_params(("parallel",)),
        name="ffn",
    )(x, *(mixer or ()), *ff)


def _log_sigmoid(x):
    return jnp.minimum(x, 0.0) - jnp.log(1.0 + jnp.exp(-jnp.abs(x)))


def _lane_scan(x, op, fill):
    n = x.shape[1]
    idx = lax.broadcasted_iota(jnp.int32, x.shape, 1)
    k = 1
    while k < n:
        x = op(x, jnp.where(idx >= k, pltpu.roll(x, k, axis=1), fill))
        k *= 2
    return x


def _rope_lanes(x, cos, sin_up, sin_dn):
    half = A_ROPE // 2
    return (x * cos + pltpu.roll(x, half, axis=1) * sin_up
            + pltpu.roll(x, LANES - half, axis=1) * sin_dn)


def _proj_kernel(x_ref, pre_g, w_in, brow_ref, q_a_g, w_uq, kv_a_g, cos_ref, sup_ref, sdn_ref,
                 mq_ref, mk_ref, mv_ref, mo_ref, aux_ref, gst_ref, q_ref, ckv_ref, *, block):
    h = _rms(x_ref[...], pre_g[...]).astype(BF16)
    cos, sup, sdn = cos_ref[...], sup_ref[...], sdn_ref[...]
    aux = _rope_lanes(jnp.dot(h, w_in[:, COL_AUX:], preferred_element_type=F32), cos, sup, sdn)
    aux_ref[...] = aux
    pre = aux.T[:2 * M_HEADS, :] + brow_ref[...]
    for c in range(pre.shape[1] // block):
        cs = slice(c * block, (c + 1) * block)
        b_all = pltpu.roll(_lane_scan(_log_sigmoid(pre[:, cs]), jnp.add, 0.0), M_HEADS, axis=0)
        a_all = pre[:, cs] - b_all
        gst_ref[GST_A:GST_A + SUBLANES, cs] = a_all
        gst_ref[GST_B:GST_B + SUBLANES, cs] = b_all
        gst_ref[GST_CMAX:GST_CMAX + SUBLANES, cs] = _lane_scan(a_all, jnp.maximum, -jnp.inf)
    aq = jnp.dot(h, w_in[:, COL_AQ:COL_AKV], preferred_element_type=F32)
    q = jnp.dot(_rms(aq, q_a_g[...]).astype(BF16), w_uq[...], preferred_element_type=F32)
    q_scale = A_SCALE * math.log2(math.e)
    for hd in range(A_HEADS):
        sl = slice(hd * LANES, (hd + 1) * LANES)
        q_ref[:, sl] = (_rope_lanes(q[:, sl], cos, sup, sdn) * q_scale).astype(BF16)
    p = jnp.dot(h, w_in[:, :COL_AQ], preferred_element_type=F32)
    mq_ref[...] = p[:, COL_MQ:COL_MQ + M_WIDTH].astype(BF16)
    mk_ref[...] = (p[:, COL_MK:COL_MK + M_WIDTH] * (M_HEAD_DIM ** -0.5)).astype(BF16)
    mv_ref[...] = p[:, COL_MV:COL_MV + M_WIDTH].astype(BF16)
    mo_ref[...] = p[:, COL_MO:COL_MO + M_WIDTH].astype(BF16)
    akv = jnp.dot(h, w_in[:, COL_AKV:COL_AUX], preferred_element_type=F32)
    ckv_ref[...] = _rms(akv, kv_a_g[...])


def _proj(x, tables, pre_g, w_in, bias_row, q_a_g, w_uq, kv_a_g, block):
    b, s, _ = x.shape
    tm = _row_tile(s, 512, LANES)
    tok = lambda w: pl.BlockSpec((None, tm, w), lambda bi, j: (bi, j, 0))
    tok_t = lambda w: pl.BlockSpec((None, w, tm), lambda bi, j: (bi, 0, j))
    tab = pl.BlockSpec((tm, LANES), lambda bi, j: (j, 0))
    out_shape = [jax.ShapeDtypeStruct((b, s, M_WIDTH), BF16),
                 jax.ShapeDtypeStruct((b, s, M_WIDTH), BF16),
                 jax.ShapeDtypeStruct((b, s, M_WIDTH), BF16),
                 jax.ShapeDtypeStruct((b, s, M_WIDTH), BF16),
                 jax.ShapeDtypeStruct((b, s, LANES), F32),
                 jax.ShapeDtypeStruct((b, GST_ROWS, s), F32),
                 jax.ShapeDtypeStruct((b, s, A_KPAD), BF16),
                 jax.ShapeDtypeStruct((b, s, KV_LORA), F32)]
    out_specs = [tok(M_WIDTH), tok(M_WIDTH), tok(M_WIDTH), tok(M_WIDTH), tok(LANES),
                 tok_t(GST_ROWS), tok(A_KPAD), tok(KV_LORA)]
    return pl.pallas_call(
        functools.partial(_proj_kernel, block=block),
        grid=(b, s // tm),
        in_specs=[tok(D_MODEL), _const_spec((1, D_MODEL)), _const_spec((D_MODEL, IN_PAD)),
                  _const_spec((2 * M_HEADS, 1)),
                  _const_spec((1, Q_LORA)), _const_spec((Q_LORA, A_KPAD)),
                  _const_spec((1, KV_LORA)), tab, tab, tab],
        out_specs=out_specs,
        out_shape=out_shape,
        compiler_params=_params(("parallel", "parallel")),
        name="proj",
    )(x, pre_g, w_in, bias_row, q_a_g, w_uq, kv_a_g, *tables)


def _expand_kernel(c_ref, aux_ref, wk, wv_t, k_ref, vt_ref):
    cb = c_ref[...].astype(BF16)
    aux = aux_ref[...]
    lane = lax.broadcasted_iota(jnp.int32, aux.shape, 1)
    k_rope = jnp.where(lane >= AUX_ROPE_LANE, aux, 0.0)
    kn = jnp.dot(cb, wk[...], preferred_element_type=F32)
    for hd in range(A_HEADS):
        sl = slice(hd * LANES, (hd + 1) * LANES)
        k_ref[:, sl] = (kn[:, sl] + k_rope).astype(BF16)
    vt_ref[...] = lax.dot_general(wv_t[...], cb, NT_DIMS,
                                  preferred_element_type=F32).astype(BF16)


def _expand(c, aux, wk, wv_t):
    b, t, _ = c.shape
    tm = _row_tile(t, 1024, LANES)
    tok = lambda w: pl.BlockSpec((None, tm, w), lambda bi, j: (bi, j, 0))
    return pl.pallas_call(
        _expand_kernel,
        grid=(b, t // tm),
        in_specs=[tok(KV_LORA), tok(LANES), _const_spec((KV_LORA, A_KPAD)),
                  _const_spec((A_WIDTH, KV_LORA))],
        out_specs=[tok(A_KPAD), pl.BlockSpec((None, A_WIDTH, tm), lambda bi, j: (bi, 0, j))],
        out_shape=[jax.ShapeDtypeStruct((b, t, A_KPAD), BF16),
                   jax.ShapeDtypeStruct((b, A_WIDTH, t), BF16)],
        compiler_params=_params(("parallel", "parallel")),
        name="expand",
    )(c, aux, wk, wv_t)


def _mlstm_kernel(mq_ref, mk_ref, mv_ref, mo_ref, gst_ref, hg_ref,
                  c0_ref, n0_ref, m0_ref, hm_ref, c_ref, n_ref, m_ref):
    @pl.when(pl.program_id(1) == 0)
    def _():
        c_ref[...] = c0_ref[...]
        n_ref[...] = n0_ref[...]
        m_ref[...] = m0_ref[...]

    ln = mq_ref.shape[0]
    a_all = gst_ref[GST_A:GST_A + SUBLANES, :]
    b_all = gst_ref[GST_B:GST_B + SUBLANES, :]
    m_all = m_ref[...][:, :1]
    g_all = jnp.maximum(m_all, gst_ref[GST_CMAX:GST_CMAX + SUBLANES, :])
    a_cols = jnp.concatenate([a_all, jnp.zeros((LANES - SUBLANES, ln), F32)], axis=0).T
    src = lax.broadcasted_iota(jnp.int32, (ln, ln), 0)
    tgt = lax.broadcasted_iota(jnp.int32, (ln, ln), 1)
    causal = src <= tgt

    def outputs(hd):
        sl = slice(hd * M_HEAD_DIM, (hd + 1) * M_HEAD_DIM)
        q, k, v = mq_ref[:, sl], mk_ref[:, sl], mv_ref[:, sl]
        m_prev = m_all[hd:hd + 1, :]
        b_row, g_row = b_all[hd:hd + 1, :], g_all[hd:hd + 1, :]
        d = jnp.where(causal, jnp.exp(a_cols[:, hd:hd + 1] - g_row), 0.0)
        s = lax.dot_general(k, q, NT_DIMS, preferred_element_type=F32) * d
        w_carry = jnp.exp(m_prev - g_row)
        cq = lax.dot_general(c_ref[hd].astype(BF16), q, NT_DIMS, preferred_element_type=F32)
        num = lax.dot_general(v, s.astype(BF16), TN_DIMS,
                              preferred_element_type=F32) + w_carry * cq
        nq = lax.dot_general(n_ref[hd].astype(BF16), q, NT_DIMS, preferred_element_type=F32)[:1, :]
        den = jnp.sum(s, axis=0, keepdims=True) + w_carry * nq
        den = jnp.maximum(jnp.abs(den), jnp.exp(-(b_row + g_row)))
        hh = num / den
        return hh * lax.rsqrt(jnp.mean(hh * hh, axis=0, keepdims=True) + EPS)

    def finish(hd, hh):
        sl = slice(hd * M_HEAD_DIM, (hd + 1) * M_HEAD_DIM)
        k, v = mk_ref[:, sl], mv_ref[:, sl]
        gate = jax.nn.sigmoid(mo_ref[:, sl].astype(F32))
        hm_ref[:, sl] = (hh.T * hg_ref[:, sl] * gate).astype(BF16)
        m_prev = m_all[hd:hd + 1, :]
        g_last = g_all[hd:hd + 1, ln - 1:ln]
        w_state = jnp.exp(m_prev - g_last)
        kw = k.astype(F32) * jnp.exp(a_cols[:, hd:hd + 1] - g_last)
        c_ref[hd] = w_state * c_ref[hd] + lax.dot_general(v, kw.astype(BF16), TN_DIMS,
                                                          preferred_element_type=F32)
        n_ref[hd] = w_state * n_ref[hd] + jnp.sum(kw, axis=0, keepdims=True)
        m_ref[hd:hd + 1, :] = jnp.broadcast_to(b_all[hd:hd + 1, ln - 1:ln] + g_last, (1, LANES))

    hh_next = outputs(0)
    for hd in range(M_HEADS):
        hh, hh_next = hh_next, (outputs(hd + 1) if hd + 1 < M_HEADS else None)
        finish(hd, hh)


def _mlstm(mq, mk, mv, mo, gst, head_g, c0, n0, m0, block):
    b, s, _ = mq.shape
    tok = lambda w: pl.BlockSpec((None, block, w), lambda bi, j: (bi, j, 0))
    tok_t = lambda w: pl.BlockSpec((None, w, block), lambda bi, j: (bi, 0, j))
    st_c = pl.BlockSpec((None, M_HEADS, M_HEAD_DIM, M_HEAD_DIM), lambda bi, j: (bi, 0, 0, 0))
    st_n = pl.BlockSpec((None, M_HEADS, SUBLANES, M_HEAD_DIM), lambda bi, j: (bi, 0, 0, 0))
    st_m = pl.BlockSpec((None, SUBLANES, LANES), lambda bi, j: (bi, 0, 0))
    return pl.pallas_call(
        _mlstm_kernel,
        grid=(b, s // block),
        in_specs=[tok(M_WIDTH), tok(M_WIDTH), tok(M_WIDTH), tok(M_WIDTH), tok_t(GST_ROWS),
                  _const_spec((1, M_WIDTH)), st_c, st_n, st_m],
        out_specs=[tok(M_WIDTH), st_c, st_n, st_m],
        out_shape=[jax.ShapeDtypeStruct((b, s, M_WIDTH), BF16),
                   jax.ShapeDtypeStruct((b, M_HEADS, M_HEAD_DIM, M_HEAD_DIM), F32),
                   jax.ShapeDtypeStruct((b, M_HEADS, SUBLANES, M_HEAD_DIM), F32),
                   jax.ShapeDtypeStruct((b, SUBLANES, LANES), F32)],
        compiler_params=_params(("parallel", "arbitrary")),
        name="mlstm",
    )(mq, mk, mv, mo, gst, head_g, c0, n0, m0)


def _attn_kernel(qi_ref, ki_ref, last_ref, diag_ref, q_ref, k_ref, vt_ref, o_ref,
                 m_scr, acc_scr, *, has_diag):
    t = pl.program_id(1)
    ki = ki_ref[t]
    tq, tk = q_ref.shape[0], k_ref.shape[0]

    @pl.when(ki == 0)
    def _():
        m_scr[...] = jnp.full(m_scr.shape, -jnp.inf, F32)
        acc_scr[...] = jnp.zeros(acc_scr.shape, F32)

    def step(diag):
        panel = min(A_PANEL_DIAG if diag else A_PANEL, tq)
        if diag:
            k_chunk = lax.broadcasted_iota(jnp.int32, (panel, panel), 0) >> CHUNK_SHIFT
            q_chunk = lax.broadcasted_iota(jnp.int32, (panel, panel), 1) >> CHUNK_SHIFT
            visible = q_chunk >= k_chunk
        ones = jnp.ones((A_ONES, tk), BF16)

        def scores(hd, c):
            sl, cs = slice(hd * LANES, (hd + 1) * LANES), slice(c, c + panel)
            n_keys = c + panel if diag else tk
            s = lax.dot_general(k_ref[:n_keys, sl], q_ref[cs, sl], NT_DIMS,
                                preferred_element_type=F32)
            if diag:
                square = jnp.where(visible, s[c:, :], -jnp.inf)
                s = jnp.concatenate([s[:c, :], square], axis=0) if c else square
            m_prev = m_scr[hd, :, cs]
            m_new = jnp.maximum(m_prev, jnp.max(s, axis=0, keepdims=True))
            m_scr[hd, :, cs] = m_new
            return s, m_new, jnp.exp2(m_prev - m_new)

        def probs(s, m_new):
            return jnp.exp2(s - m_new).astype(BF16)

        def accumulate(hd, c, p, alpha):
            n_keys, cs = p.shape[0], slice(c, c + panel)
            vt = jnp.concatenate([vt_ref[hd * A_V:(hd + 1) * A_V, :n_keys], ones[:, :n_keys]], axis=0)
            acc_scr[hd, :, cs] = alpha * acc_scr[hd, :, cs] + jnp.dot(vt, p, preferred_element_type=F32)

        units = [(hd, c) for hd in range(A_HEADS) for c in range(0, tq, panel)]
        st_s, st_p = {}, {}
        for i in range(len(units) + 2):
            if i < len(units):
                st_s[i] = scores(*units[i])
            if 0 <= i - 1 < len(units):
                s, m_new, alpha = st_s.pop(i - 1)
                st_p[i - 1] = (probs(s, m_new), alpha)
            if 0 <= i - 2 < len(units):
                accumulate(*units[i - 2], *st_p.pop(i - 2))

    if has_diag:
        pl.when(diag_ref[t] == 1)(lambda: step(True))
    pl.when(diag_ref[t] == 0)(lambda: step(False))

    @pl.when(last_ref[t] == 1)
    def _():
        out_t = jnp.concatenate([acc_scr[hd][:A_V, :] / acc_scr[hd][A_V:A_V + 1, :]
                                 for hd in range(A_HEADS)], axis=0)
        o_ref[...] = out_t.T.astype(BF16)


def _attn(q, k, vt, q_off, tq, tk):
    b, sq, _ = q.shape
    sk = k.shape[1]
    nq, nk = sq // tq, sk // tk
    qi, ki, last, diag = [], [], [], []
    for i in range(nq):
        q_first, q_last = q_off + i * tq, q_off + i * tq + tq - 1
        j_end = min(nk - 1, (((q_last >> CHUNK_SHIFT) + 1) * CHUNK - 1) // tk)
        for j in range(j_end + 1):
            needs_mask = ((j * tk + tk - 1) >> CHUNK_SHIFT) > (q_first >> CHUNK_SHIFT)
            if needs_mask and not (tq == tk and q_first == j * tk and tk % min(A_PANEL_DIAG, tq) == 0):
                raise NotImplementedError("masked attention tiles must be diagonal and panel aligned")
            qi.append(i)
            ki.append(j)
            last.append(int(j == j_end))
            diag.append(int(needs_mask))
    arrs = [jnp.asarray(np.asarray(a, np.int32)) for a in (qi, ki, last, diag)]
    grid_spec = pltpu.PrefetchScalarGridSpec(
        num_scalar_prefetch=4,
        grid=(b, len(qi)),
        in_specs=[pl.BlockSpec((None, tq, A_KPAD), lambda bi, t, qi, ki, la, dg: (bi, qi[t], 0)),
                  pl.BlockSpec((None, tk, A_KPAD), lambda bi, t, qi, ki, la, dg: (bi, ki[t], 0)),
                  pl.BlockSpec((None, A_WIDTH, tk), lambda bi, t, qi, ki, la, dg: (bi, 0, ki[t]))],
        out_specs=pl.BlockSpec((None, tq, A_WIDTH), lambda bi, t, qi, ki, la, dg: (bi, qi[t], 0)),
        scratch_shapes=[pltpu.VMEM((A_HEADS, 1, tq), F32),
                        pltpu.VMEM((A_HEADS, A_V + A_ONES, tq), F32)],
    )
    return pl.pallas_call(
        functools.partial(_attn_kernel, has_diag=any(diag)),
        grid_spec=grid_spec,
        out_shape=jax.ShapeDtypeStruct((b, sq, A_WIDTH), BF16),
        compiler_params=_params(("parallel", "arbitrary")),
        name="attn",
    )(*arrs, q, k, vt)


def _rope_tables(pos):
    half = A_ROPE // 2
    freq = ROPE_THETA ** (-jnp.arange(half, dtype=F32) / half)
    ang = pos.astype(F32)[:, None] * freq[None, :]
    cos, sin = jnp.cos(ang), jnp.sin(ang)
    n = pos.shape[0]
    ones = jnp.ones((n, A_NOPE), F32)
    z = lambda w: jnp.zeros((n, w), F32)
    tail = LANES - A_NOPE - A_ROPE
    cos_t = jnp.concatenate([ones, cos, cos, z(tail)], axis=1)
    sin_up = jnp.concatenate([z(A_NOPE + half), sin, z(tail)], axis=1)
    sin_dn = jnp.concatenate([z(A_NOPE), -sin, z(half + tail)], axis=1)
    return cos_t, sin_up, sin_dn


def _pad_heads(w, widths, n_heads):
    kdim = w.shape[0]
    w = w.reshape(kdim, n_heads, sum(widths))
    w = jnp.pad(w, ((0, 0), (0, 0), (0, LANES - sum(widths))))
    return w.reshape(kdim, n_heads * LANES)


def _layer(x, pos, ff1, mix, ff2, w_ukv_k, w_ukv_vt, cache, mlstm_block, attn_tiles):
    b, s, _ = x.shape
    mix_pre_g, w_in_p, bias_row, q_a_g, w_uq_p, kv_a_g, head_g, w_out, mix_post_g = mix
    x1 = _ffn(x.reshape(b * s, D_MODEL), ff1).reshape(b, s, D_MODEL)
    mq, mk, mv, mo, aux, gst, q, c_kv = _proj(x1, _rope_tables(pos), mix_pre_g, w_in_p, bias_row,
                                              q_a_g, w_uq_p, kv_a_g, mlstm_block)
    k_rope = aux[:, :, AUX_ROPE_LANE:AUX_ROPE_LANE + A_ROPE]
    if cache is None:
        c_all, aux_all, q_off = c_kv, aux, 0
        c0 = jnp.zeros((b, M_HEADS, M_HEAD_DIM, M_HEAD_DIM), F32)
        n0 = jnp.zeros((b, M_HEADS, SUBLANES, M_HEAD_DIM), F32)
        m0 = jnp.zeros((b, SUBLANES, LANES), F32)
    else:
        cache_kv, cache_kr, c0, n0, m0 = cache
        q_off = cache_kv.shape[1]
        c_all = jnp.concatenate([cache_kv, c_kv], axis=1)
        kr_pad = jnp.pad(cache_kr, ((0, 0), (0, 0), (AUX_ROPE_LANE, LANES - AUX_ROPE_LANE - A_ROPE)))
        aux_all = jnp.concatenate([kr_pad, aux], axis=1)
        n0 = jnp.pad(n0[:, :, None, :], ((0, 0), (0, 0), (0, SUBLANES - 1), (0, 0)))
        m0 = jnp.broadcast_to(jnp.pad(m0, ((0, 0), (0, SUBLANES - M_HEADS)))[:, :, None],
                              (b, SUBLANES, LANES))
    k, vt = _expand(c_all, aux_all, w_ukv_k, w_ukv_vt)
    hm, c_new, n_new, m_new = _mlstm(mq, mk, mv, mo, gst, head_g, c0, n0, m0, mlstm_block)
    tq, tk = attn_tiles
    at = _attn(q, k, vt, q_off, tq, min(tk, k.shape[1]))
    n_tok = b * s
    mixer = (hm.reshape(n_tok, M_WIDTH), at.reshape(n_tok, A_WIDTH), w_out, mix_post_g)
    y = _ffn(x1.reshape(n_tok, D_MODEL), ff2, mixer).reshape(b, s, D_MODEL)
    state = (c_kv, k_rope, c_new, n_new[:, :, 0, :], m_new[:, :M_HEADS, 0])
    return y, state


def kernel(x_prompt, x_sample, cache_kv_latent, cache_k_rope, state_C, state_n, state_m, ff1_pre_g, ff1_w_gate, ff1_w_up, ff1_w_down, ff1_post_g, mix_pre_g, w_in, b_igate, b_fgate, q_a_g, w_uq, kv_a_g, w_ukv, m_head_g, w_out, mix_post_g, ff2_pre_g, ff2_w_gate, ff2_w_up, ff2_w_down, ff2_post_g):
    depth = w_in.shape[0]
    y_p, y_s = x_prompt, x_sample
    pos_p = jnp.arange(x_prompt.shape[1])
    pos_s = cache_kv_latent.shape[2] + jnp.arange(x_sample.shape[1])
    new_p, new_s = [], []
    row = lambda g: g.reshape(1, -1)
    for l in range(depth):
        ff1 = (row(ff1_pre_g[l]), ff1_w_gate[l].astype(BF16), ff1_w_up[l].astype(BF16),
               ff1_w_down[l].astype(BF16), row(ff1_post_g[l]))
        ff2 = (row(ff2_pre_g[l]), ff2_w_gate[l].astype(BF16), ff2_w_up[l].astype(BF16),
               ff2_w_down[l].astype(BF16), row(ff2_post_g[l]))
        wl = w_in[l]
        n_m = 4 * M_WIDTH
        gates = wl[:, n_m:n_m + 2 * M_HEADS]
        aq_akv = wl[:, n_m + 2 * M_HEADS:n_m + 2 * M_HEADS + Q_LORA + KV_LORA]
        ar = wl[:, n_m + 2 * M_HEADS + Q_LORA + KV_LORA:]
        zc = lambda w: jnp.zeros((D_MODEL, w), F32)
        aux_cols = jnp.concatenate([gates, zc(AUX_ROPE_LANE - 2 * M_HEADS), ar,
                                    zc(LANES - AUX_ROPE_LANE - A_ROPE)], axis=1)
        w_in_p = jnp.concatenate([wl[:, :n_m], aq_akv, aux_cols], axis=1).astype(BF16)
        bias_row = jnp.concatenate([b_igate[l], b_fgate[l]]).reshape(2 * M_HEADS, 1)
        w_uq_p = _pad_heads(w_uq[l], (A_NOPE, A_ROPE), A_HEADS).astype(BF16)
        wkv = w_ukv[l].reshape(KV_LORA, A_HEADS, A_NOPE + A_V)
        w_ukv_k = _pad_heads(wkv[:, :, :A_NOPE].reshape(KV_LORA, A_HEADS * A_NOPE), (A_NOPE,),
                             A_HEADS).astype(BF16)
        w_ukv_vt = wkv[:, :, A_NOPE:].reshape(KV_LORA, A_WIDTH).T.astype(BF16)
        mix = (row(mix_pre_g[l]), w_in_p, bias_row, row(q_a_g[l]), w_uq_p,
               row(kv_a_g[l]), row(m_head_g[l]), w_out[l].astype(BF16), row(mix_post_g[l]))
        y_p, st_p = _layer(y_p, pos_p, ff1, mix, ff2, w_ukv_k, w_ukv_vt, None,
                           mlstm_block=256, attn_tiles=(1024, 1024))
        cache = (cache_kv_latent[l], cache_k_rope[l], state_C[l], state_n[l], state_m[l])
        y_s, st_s = _layer(y_s, pos_s, ff1, mix, ff2, w_ukv_k, w_ukv_vt, cache,
                           mlstm_block=x_sample.shape[1], attn_tiles=(x_sample.shape[1], 4096))
        new_p.append(st_p)
        new_s.append(st_s)
    outs_p = [jnp.stack(a) for a in zip(*new_p)]
    outs_s = [jnp.stack(a) for a in zip(*new_s)]
    return (y_p, y_s, *outs_p, *outs_s)
```

```python
import functools
import math

import numpy as np
import jax
import jax.numpy as jnp
from jax import lax
from jax.experimental import pallas as pl
from jax.experimental.pallas import tpu as pltpu

F32 = jnp.float32
BF16 = jnp.bfloat16

D_MODEL = 1024
CHUNK = 64
CHUNK_SHIFT = 6
EPS = 1e-6
M_HEADS = 4
M_HEAD_DIM = 128
M_WIDTH = M_HEADS * M_HEAD_DIM
A_HEADS = 8
A_NOPE = 64
A_ROPE = 32
A_V = 64
A_WIDTH = A_HEADS * A_V
Q_LORA = 384
KV_LORA = 256
ROPE_THETA = 10000.0
A_SCALE = (A_NOPE + A_ROPE) ** -0.5
D_FF = 2816

LANES = 128
SUBLANES = 8
A_KPAD = A_HEADS * LANES
A_PANEL = 512
A_PANEL_DIAG = 256
M_STREAMS = 1
FFN_TILE, FFN_SUB = 1024, 256
A_ONES = 16
COL_MQ, COL_MK, COL_MV, COL_MO = 0, M_WIDTH, 2 * M_WIDTH, 3 * M_WIDTH
COL_AQ = 4 * M_WIDTH
COL_AKV = COL_AQ + Q_LORA
COL_AUX = COL_AKV + KV_LORA
IN_PAD = COL_AUX + LANES
AUX_ROPE_LANE = A_NOPE
GST_A, GST_B, GST_CMAX, GST_ROWS = 0, SUBLANES, 2 * SUBLANES, 3 * SUBLANES
VMEM_LIMIT = 56 * 1024 * 1024
NT_DIMS = (((1,), (1,)), ((), ()))
TN_DIMS = (((0,), (0,)), ((), ()))


def _rms(x, g):
    return x * lax.rsqrt(jnp.mean(x * x, axis=-1, keepdims=True) + EPS) * g


def _const_spec(shape):
    nd = len(shape)
    return pl.BlockSpec(shape, lambda *_: (0,) * nd, pipeline_mode=pl.Buffered(1))


def _params(sem):
    return pltpu.CompilerParams(dimension_semantics=sem, vmem_limit_bytes=VMEM_LIMIT)


def _row_tile(n, cap, mult=16):
    best = n
    for t in range(mult, min(n, cap) + 1, mult):
        if n % t == 0:
            best = t
    return best


def _ffn_kernel(x_ref, *refs, mixer):
    if mixer:
        hm_ref, at_ref, w_out, mix_g, *refs = refs
    pre_g, wg, wu, wd, post_g, o_ref = refs
    sub = min(FFN_SUB, x_ref.shape[0])
    for r in range(0, x_ref.shape[0], sub):
        rows = slice(r, r + sub)
        x = x_ref[rows, :]
        if mixer:
            merged = jnp.concatenate([hm_ref[rows, :], at_ref[rows, :]], axis=-1)
            x = x + _rms(jnp.dot(merged, w_out[...], preferred_element_type=F32), mix_g[...])
        h = _rms(x, pre_g[...]).astype(BF16)
        g = jnp.dot(h, wg[...], preferred_element_type=F32)
        u = jnp.dot(h, wu[...], preferred_element_type=F32)
        a = (g * jax.nn.sigmoid(g) * u).astype(BF16)
        y = jnp.dot(a, wd[...], preferred_element_type=F32)
        o_ref[rows, :] = x + 0.5 * _rms(y, post_g[...])


def _ffn(x, ff, mixer=None):
    n = x.shape[0]
    tm = _row_tile(n, FFN_TILE)
    tok = lambda w: pl.BlockSpec((tm, w), lambda i: (i, 0))
    mix_specs =[] if mixer is None else [tok(M_WIDTH), tok(A_WIDTH),
                                          _const_spec((M_WIDTH + A_WIDTH, D_MODEL)),
                                          _const_spec((1, D_MODEL))]
    return pl.pallas_call(
        functools.partial(_ffn_kernel, mixer=mixer is not None),
        grid=(n // tm,),
        in_specs=[tok(D_MODEL)] + mix_specs
        + [_const_spec((1, D_MODEL)), _const_spec((D_MODEL, D_FF)), _const_spec((D_MODEL, D_FF)),
           _const_spec((D_FF, D_MODEL)), _const_spec((1, D_MODEL))],
        out_specs=tok(D_MODEL),
        out_shape=jax.ShapeDtypeStruct((n, D_MODEL), F32),
        compiler_params=_params(("parallel",)),
        name="ffn",
    )(x, *(mixer or ()), *ff)


def _log_sigmoid(x):
    return jnp.minimum(x, 0.0) - jnp.log(1.0 + jnp.exp(-jnp.abs(x)))


def _lane_scan(x, op, fill):
    n = x.shape[1]
    idx = lax.broadcasted_iota(jnp.int32, x.shape, 1)
    k = 1
    while k < n:
        x = op(x, jnp.where(idx >= k, pltpu.roll(x, k, axis=1), fill))
        k *= 2
    return x


def _rope_lanes(x, cos, sin_up, sin_dn):
    half = A_ROPE // 2
    return (x * cos + pltpu.roll(x, half, axis=1) * sin_up
            + pltpu.roll(x, LANES - half, axis=1) * sin_dn)


def _proj_kernel(x_ref, pre_g, w_in, brow_ref, q_a_g, w_uq, kv_a_g, cos_ref, sup_ref, sdn_ref,
                 mq_ref, mk_ref, mv_ref, mo_ref, aux_ref, gst_ref, q_ref, ckv_ref, kr_ref, *, block):
    h = _rms(x_ref[...], pre_g[...]).astype(BF16)
    cos, sup, sdn = cos_ref[...], sup_ref[...], sdn_ref[...]
    aux = _rope_lanes(jnp.dot(h, w_in[:, COL_AUX:], preferred_element_type=F32), cos, sup, sdn)
    aux_ref[...] = aux
    kr_ref[...] = aux[:, AUX_ROPE_LANE:AUX_ROPE_LANE + A_ROPE]
    pre = aux.T[:2 * M_HEADS, :] + brow_ref[...]
    for c in range(pre.shape[1] // block):
        cs = slice(c * block, (c + 1) * block)
        b_all = pltpu.roll(_lane_scan(_log_sigmoid(pre[:, cs]), jnp.add, 0.0), M_HEADS, axis=0)
        a_all = pre[:, cs] - b_all
        gst_ref[GST_A:GST_A + SUBLANES, cs] = a_all
        gst_ref[GST_B:GST_B + SUBLANES, cs] = b_all
        gst_ref[GST_CMAX:GST_CMAX + SUBLANES, cs] = _lane_scan(a_all, jnp.maximum, -jnp.inf)
    aq = jnp.dot(h, w_in[:, COL_AQ:COL_AKV], preferred_element_type=F32)
    q = jnp.dot(_rms(aq, q_a_g[...]).astype(BF16), w_uq[...], preferred_element_type=F32)
    q_scale = A_SCALE * math.log2(math.e)
    for hd in range(A_HEADS):
        sl = slice(hd * LANES, (hd + 1) * LANES)
        q_ref[:, sl] = (_rope_lanes(q[:, sl], cos, sup, sdn) * q_scale).astype(BF16)
    p = jnp.dot(h, w_in[:, :COL_AQ], preferred_element_type=F32)
    mq_ref[...] = p[:, COL_MQ:COL_MQ + M_WIDTH].astype(BF16)
    mk_ref[...] = (p[:, COL_MK:COL_MK + M_WIDTH] * (M_HEAD_DIM ** -0.5)).astype(BF16)
    mv_ref[...] = p[:, COL_MV:COL_MV + M_WIDTH].astype(BF16)
    mo_ref[...] = p[:, COL_MO:COL_MO + M_WIDTH].astype(BF16)
    akv = jnp.dot(h, w_in[:, COL_AKV:COL_AUX], preferred_element_type=F32)
    ckv_ref[...] = _rms(akv, kv_a_g[...])


def _proj(x, tables, pre_g, w_in, bias_row, q_a_g, w_uq, kv_a_g, block):
    b, s, _ = x.shape
    tm = _row_tile(s, 512, LANES)
    tok = lambda w: pl.BlockSpec((None, tm, w), lambda bi, j: (bi, j, 0))
    tok_t = lambda w: pl.BlockSpec((None, w, tm), lambda bi, j: (bi, 0, j))
    tab = pl.BlockSpec((tm, LANES), lambda bi, j: (j, 0))
    out_shape = [jax.ShapeDtypeStruct((b, s, M_WIDTH), BF16),
                 jax.ShapeDtypeStruct((b, s, M_WIDTH), BF16),
                 jax.ShapeDtypeStruct((b, s, M_WIDTH), BF16),
                 jax.ShapeDtypeStruct((b, s, M_WIDTH), BF16),
                 jax.ShapeDtypeStruct((b, s, LANES), F32),
                 jax.ShapeDtypeStruct((b, GST_ROWS, s), F32),
                 jax.ShapeDtypeStruct((b, s, A_KPAD), BF16),
                 jax.ShapeDtypeStruct((b, s, KV_LORA), F32),
                 jax.ShapeDtypeStruct((b, s, A_ROPE), F32)]
    out_specs = [tok(M_WIDTH), tok(M_WIDTH), tok(M_WIDTH), tok(M_WIDTH), tok(LANES),
                 tok_t(GST_ROWS), tok(A_KPAD), tok(KV_LORA), tok(A_ROPE)]
    return pl.pallas_call(
        functools.partial(_proj_kernel, block=block),
        grid=(b, s // tm),
        in_specs=[tok(D_MODEL), _const_spec((1, D_MODEL)), _const_spec((D_MODEL, IN_PAD)),
                  _const_spec((2 * M_HEADS, 1)),
                  _const_spec((1, Q_LORA)), _const_spec((Q_LORA, A_KPAD)),
                  _const_spec((1, KV_LORA)), tab, tab, tab],
        out_specs=out_specs,
        out_shape=out_shape,
        compiler_params=_params(("parallel", "parallel")),
        name="proj",
    )(x, pre_g, w_in, bias_row, q_a_g, w_uq, kv_a_g, *tables)


def _expand_kernel(c_ref, aux_ref, wk, wv_t, k_ref, vt_ref):
    cb = c_ref[...].astype(BF16)
    aux = aux_ref[...]
    lane = lax.broadcasted_iota(jnp.int32, aux.shape, 1)
    k_rope = jnp.where(lane >= AUX_ROPE_LANE, aux, 0.0)
    kn = jnp.dot(cb, wk[...], preferred_element_type=F32)
    for hd in range(A_HEADS):
        sl = slice(hd * LANES, (hd + 1) * LANES)
        k_ref[:, sl] = (kn[:, sl] + k_rope).astype(BF16)
    vt_ref[...] = lax.dot_general(wv_t[...], cb, NT_DIMS,
                                  preferred_element_type=F32).astype(BF16)


def _expand(c, aux, wk, wv_t):
    b, t, _ = c.shape
    tm = _row_tile(t, 1024, LANES)
    tok = lambda w: pl.BlockSpec((None, tm, w), lambda bi, j: (bi, j, 0))
    return pl.pallas_call(
        _expand_kernel,
        grid=(b, t // tm),
        in_specs=[tok(KV_LORA), tok(LANES), _const_spec((KV_LORA, A_KPAD)),
                  _const_spec((A_WIDTH, KV_LORA))],
        out_specs=[tok(A_KPAD), pl.BlockSpec((None, A_WIDTH, tm), lambda bi, j: (bi, 0, j))],
        out_shape=[jax.ShapeDtypeStruct((b, t, A_KPAD), BF16),
                   jax.ShapeDtypeStruct((b, A_WIDTH, t), BF16)],
        compiler_params=_params(("parallel", "parallel")),
        name="expand",
    )(c, aux, wk, wv_t)


def _mlstm_kernel(mq_ref, mk_ref, mv_ref, mo_ref, gst_ref, hg_ref,
                  c0_ref, n0_ref, m0_ref, hm_ref, c_ref, n_ref, m_ref):
    @pl.when(pl.program_id(1) == 0)
    def _():
        c_ref[...] = c0_ref[...]
        n_ref[...] = n0_ref[...]
        m_ref[...] = m0_ref[...]

    nb, ln = mq_ref.shape[0], mq_ref.shape[1]
    src = lax.broadcasted_iota(jnp.int32, (ln, ln), 0)
    tgt = lax.broadcasted_iota(jnp.int32, (ln, ln), 1)
    causal = src <= tgt

    def gate_stats(bi):
        a_all = gst_ref[bi, GST_A:GST_A + SUBLANES, :]
        b_all = gst_ref[bi, GST_B:GST_B + SUBLANES, :]
        m_all = m_ref[bi][:, :1]
        g_all = jnp.maximum(m_all, gst_ref[bi, GST_CMAX:GST_CMAX + SUBLANES, :])
        a_cols = jnp.concatenate([a_all, jnp.zeros((LANES - SUBLANES, ln), F32)], axis=0).T
        return a_all, b_all, m_all, g_all, a_cols

    stats = [gate_stats(bi) for bi in range(nb)]

    def outputs(bi, hd):
        _, b_all, m_all, g_all, a_cols = stats[bi]
        sl = slice(hd * M_HEAD_DIM, (hd + 1) * M_HEAD_DIM)
        q, k, v = mq_ref[bi, :, sl], mk_ref[bi, :, sl], mv_ref[bi, :, sl]
        m_prev = m_all[hd:hd + 1, :]
        b_row, g_row = b_all[hd:hd + 1, :], g_all[hd:hd + 1, :]
        d = jnp.where(causal, jnp.exp(a_cols[:, hd:hd + 1] - g_row), 0.0)
        s = lax.dot_general(k, q, NT_DIMS, preferred_element_type=F32) * d
        w_carry = jnp.exp(m_prev - g_row)
        cq = lax.dot_general(c_ref[bi, hd].astype(BF16), q, NT_DIMS, preferred_element_type=F32)
        num = lax.dot_general(v, s.astype(BF16), TN_DIMS,
                              preferred_element_type=F32) + w_carry * cq
        nq = lax.dot_general(n_ref[bi, hd].astype(BF16), q, NT_DIMS,
                             preferred_element_type=F32)[:1, :]
        den = jnp.sum(s, axis=0, keepdims=True) + w_carry * nq
        den = jnp.maximum(jnp.abs(den), jnp.exp(-(b_row + g_row)))
        hh = num / den
        return hh * lax.rsqrt(jnp.mean(hh * hh, axis=0, keepdims=True) + EPS)

    def finish(bi, hd, hh):
        _, b_all, m_all, g_all, a_cols = stats[bi]
        sl = slice(hd * M_HEAD_DIM, (hd + 1) * M_HEAD_DIM)
        k, v = mk_ref[bi, :, sl], mv_ref[bi, :, sl]
        gate = jax.nn.sigmoid(mo_ref[bi, :, sl].astype(F32))
        hm_ref[bi, :, sl] = (hh.T * hg_ref[:, sl] * gate).astype(BF16)
        m_prev = m_all[hd:hd + 1, :]
        g_last = g_all[hd:hd + 1, ln - 1:ln]
        w_state = jnp.exp(m_prev - g_last)
        kw = k.astype(F32) * jnp.exp(a_cols[:, hd:hd + 1] - g_last)
        c_ref[bi, hd] = w_state * c_ref[bi, hd] + lax.dot_general(v, kw.astype(BF16), TN_DIMS,
                                                                  preferred_element_type=F32)
        n_ref[bi, hd] = w_state * n_ref[bi, hd] + jnp.sum(kw, axis=0, keepdims=True)
        m_ref[bi, hd:hd + 1, :] = jnp.broadcast_to(b_all[hd:hd + 1, ln - 1:ln] + g_last, (1, LANES))

    units = [(bi, hd) for hd in range(M_HEADS) for bi in range(nb)]
    pending = {}
    for i in range(len(units) + nb):
        if i < len(units):
            pending[i] = outputs(*units[i])
        if 0 <= i - nb < len(units):
            finish(*units[i - nb], pending.pop(i - nb))


def _mlstm(mq, mk, mv, mo, gst, head_g, c0, n0, m0, block):
    b, s, _ = mq.shape
    nb = M_STREAMS if b % M_STREAMS == 0 else 1
    tok = lambda w: pl.BlockSpec((nb, block, w), lambda bi, j: (bi, j, 0))
    tok_t = lambda w: pl.BlockSpec((nb, w, block), lambda bi, j: (bi, 0, j))
    st_c = pl.BlockSpec((nb, M_HEADS, M_HEAD_DIM, M_HEAD_DIM), lambda bi, j: (bi, 0, 0, 0))
    st_n = pl.BlockSpec((nb, M_HEADS, SUBLANES, M_HEAD_DIM), lambda bi, j: (bi, 0, 0, 0))
    st_m = pl.BlockSpec((nb, SUBLANES, LANES), lambda bi, j: (bi, 0, 0))
    return pl.pallas_call(
        _mlstm_kernel,
        grid=(b // nb, s // block),
        in_specs=[tok(M_WIDTH), tok(M_WIDTH), tok(M_WIDTH), tok(M_WIDTH), tok_t(GST_ROWS),
                  _const_spec((1, M_WIDTH)), st_c, st_n, st_m],
        out_specs=[tok(M_WIDTH), st_c, st_n, st_m],
        out_shape=[jax.ShapeDtypeStruct((b, s, M_WIDTH), BF16),
                   jax.ShapeDtypeStruct((b, M_HEADS, M_HEAD_DIM, M_HEAD_DIM), F32),
                   jax.ShapeDtypeStruct((b, M_HEADS, SUBLANES, M_HEAD_DIM), F32),
                   jax.ShapeDtypeStruct((b, SUBLANES, LANES), F32)],
        compiler_params=_params(("parallel", "arbitrary")),
        name="mlstm",
    )(mq, mk, mv, mo, gst, head_g, c0, n0, m0)


def _attn_kernel(qi_ref, ki_ref, last_ref, diag_ref, q_ref, k_ref, vt_ref, o_ref,
                 m_scr, acc_scr, *, has_diag):
    t = pl.program_id(1)
    ki = ki_ref[t]
    tq, tk = q_ref.shape[0], k_ref.shape[0]

    @pl.when(ki == 0)
    def _():
        m_scr[...] = jnp.full(m_scr.shape, -jnp.inf, F32)
        acc_scr[...] = jnp.zeros(acc_scr.shape, F32)

    def step(diag):
        panel = min(A_PANEL_DIAG if diag else A_PANEL, tq)
        if diag:
            k_chunk = lax.broadcasted_iota(jnp.int32, (panel, panel), 0) >> CHUNK_SHIFT
            q_chunk = lax.broadcasted_iota(jnp.int32, (panel, panel), 1) >> CHUNK_SHIFT
            visible = q_chunk >= k_chunk
        ones = jnp.ones((A_ONES, tk), BF16)

        def scores(hd, c):
            sl, cs = slice(hd * LANES, (hd + 1) * LANES), slice(c, c + panel)
            n_keys = c + panel if diag else tk
            s = lax.dot_general(k_ref[:n_keys, sl], q_ref[cs, sl], NT_DIMS,
                                preferred_element_type=F32)
            if diag:
                square = jnp.where(visible, s[c:, :], -jnp.inf)
                s = jnp.concatenate([s[:c, :], square], axis=0) if c else square
            m_prev = m_scr[hd, :, cs]
            m_new = jnp.maximum(m_prev, jnp.max(s, axis=0, keepdims=True))
            m_scr[hd, :, cs] = m_new
            return s, m_new, jnp.exp2(m_prev - m_new)

        def probs(s, m_new):
            return jnp.exp2(s - m_new).astype(BF16)

        def accumulate(hd, c, p, alpha):
            n_keys, cs = p.shape[0], slice(c, c + panel)
            vt = jnp.concatenate([vt_ref[hd * A_V:(hd + 1) * A_V, :n_keys], ones[:, :n_keys]], axis=0)
            acc_scr[hd, :, cs] = alpha * acc_scr[hd, :, cs] + jnp.dot(vt, p, preferred_element_type=F32)

        units = [(hd, c) for hd in range(A_HEADS) for c in range(0, tq, panel)]
        st_s, st_p = {}, {}
        for i in range(len(units) + 2):
            if i < len(units):
                st_s[i] = scores(*units[i])
            if 0 <= i - 1 < len(units):
                s, m_new, alpha = st_s.pop(i - 1)
                st_p[i - 1] = (probs(s, m_new), alpha)
            if 0 <= i - 2 < len(units):
                accumulate(*units[i - 2], *st_p.pop(i - 2))

    if has_diag:
        pl.when(diag_ref[t] == 1)(lambda: step(True))
    pl.when(diag_ref[t] == 0)(lambda: step(False))

    @pl.when(last_ref[t] == 1)
    def _():
        out_t = jnp.concatenate([acc_scr[hd][:A_V, :] / acc_scr[hd][A_V:A_V + 1, :]
                                 for hd in range(A_HEADS)], axis=0)
        o_ref[...] = out_t.T.astype(BF16)


def _attn(q, k, vt, q_off, tq, tk):
    b, sq, _ = q.shape
    sk = k.shape[1]
    nq, nk = sq // tq, sk // tk
    qi, ki, last, diag = [], [], [], []
    for i in range(nq):
        q_first, q_last = q_off + i * tq, q_off + i * tq + tq - 1
        j_end = min(nk - 1, (((q_last >> CHUNK_SHIFT) + 1) * CHUNK - 1) // tk)
        for j in range(j_end + 1):
            needs_mask = ((j * tk + tk - 1) >> CHUNK_SHIFT) > (q_first >> CHUNK_SHIFT)
            if needs_mask and not (tq == tk and q_first == j * tk and tk % min(A_PANEL_DIAG, tq) == 0):
                raise NotImplementedError("masked attention tiles must be diagonal and panel aligned")
            qi.append(i)
            ki.append(j)
            last.append(int(j == j_end))
            diag.append(int(needs_mask))
    arrs = [jnp.asarray(np.asarray(a, np.int32)) for a in (qi, ki, last, diag)]
    grid_spec = pltpu.PrefetchScalarGridSpec(
        num_scalar_prefetch=4,
        grid=(b, len(qi)),
        in_specs=[pl.BlockSpec((None, tq, A_KPAD), lambda bi, t, qi, ki, la, dg: (bi, qi[t], 0)),
                  pl.BlockSpec((None, tk, A_KPAD), lambda bi, t, qi, ki, la, dg: (bi, ki[t], 0)),
                  pl.BlockSpec((None, A_WIDTH, tk), lambda bi, t, qi, ki, la, dg: (bi, 0, ki[t]))],
        out_specs=pl.BlockSpec((None, tq, A_WIDTH), lambda bi, t, qi, ki, la, dg: (bi, qi[t], 0)),
        scratch_shapes=[pltpu.VMEM((A_HEADS, 1, tq), F32),
                        pltpu.VMEM((A_HEADS, A_V + A_ONES, tq), F32)],
    )
    return pl.pallas_call(
        functools.partial(_attn_kernel, has_diag=any(diag)),
        grid_spec=grid_spec,
        out_shape=jax.ShapeDtypeStruct((b, sq, A_WIDTH), BF16),
        compiler_params=_params(("parallel", "arbitrary")),
        name="attn",
    )(*arrs, q, k, vt)


def _rope_tables(pos):
    half = A_ROPE // 2
    n = pos.shape[0]
    freq = ROPE_THETA ** (-jnp.arange(half, dtype=F32) / half)
    ang = pos.astype(F32)[:, None] * freq[None, :]
    if (n * half) % LANES == 0:
        ang = ang.reshape(n * half // LANES, LANES)
    cos, sin = lax.optimization_barrier((jnp.cos(ang), jnp.sin(ang)))
    cos, sin = cos.reshape(n, half), sin.reshape(n, half)
    ones = jnp.ones((n, A_NOPE), F32)
    z = lambda w: jnp.zeros((n, w), F32)
    tail = LANES - A_NOPE - A_ROPE
    cos_t = jnp.concatenate([ones, cos, cos, z(tail)], axis=1)
    sin_up = jnp.concatenate([z(A_NOPE + half), sin, z(tail)], axis=1)
    sin_dn = jnp.concatenate([z(A_NOPE), -sin, z(half + tail)], axis=1)
    return cos_t, sin_up, sin_dn


def _pad_heads(w, widths, n_heads):
    kdim = w.shape[0]
    w = w.reshape(kdim, n_heads, sum(widths))
    w = jnp.pad(w, ((0, 0), (0, 0), (0, LANES - sum(widths))))
    return w.reshape(kdim, n_heads * LANES)


def _layer(x, pos, ff1, mix, ff2, w_ukv_k, w_ukv_vt, cache, mlstm_block, attn_tiles):
    b, s, _ = x.shape
    mix_pre_g, w_in_p, bias_row, q_a_g, w_uq_p, kv_a_g, head_g, w_out, mix_post_g = mix
    x1 = _ffn(x.reshape(b * s, D_MODEL), ff1).reshape(b, s, D_MODEL)
    mq, mk, mv, mo, aux, gst, q, c_kv, k_rope = _proj(x1, _rope_tables(pos), mix_pre_g, w_in_p,
                                                      bias_row, q_a_g, w_uq_p, kv_a_g, mlstm_block)
    if cache is None:
        c_all, aux_all, q_off = c_kv, aux, 0
        c0 = jnp.zeros((b, M_HEADS, M_HEAD_DIM, M_HEAD_DIM), F32)
        n0 = jnp.zeros((b, M_HEADS, SUBLANES, M_HEAD_DIM), F32)
        m0 = jnp.zeros((b, SUBLANES, LANES), F32)
    else:
        cache_kv, cache_kr, c0, n0, m0 = cache
        q_off = cache_kv.shape[1]
        c_all = jnp.concatenate([cache_kv, c_kv], axis=1)
        kr_pad = jnp.pad(cache_kr, ((0, 0), (0, 0), (AUX_ROPE_LANE, LANES - AUX_ROPE_LANE - A_ROPE)))
        aux_all = jnp.concatenate([kr_pad, aux], axis=1)
        n0 = jnp.pad(n0[:, :, None, :], ((0, 0), (0, 0), (0, SUBLANES - 1), (0, 0)))
        m0 = jnp.broadcast_to(jnp.pad(m0, ((0, 0), (0, SUBLANES - M_HEADS)))[:, :, None],
                              (b, SUBLANES, LANES))
    k, vt = _expand(c_all, aux_all, w_ukv_k, w_ukv_vt)
    hm, c_new, n_new, m_new = _mlstm(mq, mk, mv, mo, gst, head_g, c0, n0, m0, mlstm_block)
    tq, tk = attn_tiles
    at = _attn(q, k, vt, q_off, tq, min(tk, k.shape[1]))
    n_tok = b * s
    mixer = (hm.reshape(n_tok, M_WIDTH), at.reshape(n_tok, A_WIDTH), w_out, mix_post_g)
    y = _ffn(x1.reshape(n_tok, D_MODEL), ff2, mixer).reshape(b, s, D_MODEL)
    state = (c_kv, k_rope, c_new, n_new[:, :, 0, :], m_new[:, :M_HEADS, 0])
    return y, state


def kernel(x_prompt, x_sample, cache_kv_latent, cache_k_rope, state_C, state_n, state_m, ff1_pre_g, ff1_w_gate, ff1_w_up, ff1_w_down, ff1_post_g, mix_pre_g, w_in, b_igate, b_fgate, q_a_g, w_uq, kv_a_g, w_ukv, m_head_g, w_out, mix_post_g, ff2_pre_g, ff2_w_gate, ff2_w_up, ff2_w_down, ff2_post_g):
    depth = w_in.shape[0]
    y_p, y_s = x_prompt, x_sample
    pos_p = jnp.arange(x_prompt.shape[1])
    pos_s = cache_kv_latent.shape[2] + jnp.arange(x_sample.shape[1])
    new_p, new_s = [], []
    row = lambda g: g.reshape(1, -1)
    for l in range(depth):
        ff1 = (row(ff1_pre_g[l]), ff1_w_gate[l].astype(BF16), ff1_w_up[l].astype(BF16),
               ff1_w_down[l].astype(BF16), row(ff1_post_g[l]))
        ff2 = (row(ff2_pre_g[l]), ff2_w_gate[l].astype(BF16), ff2_w_up[l].astype(BF16),
               ff2_w_down[l].astype(BF16), row(ff2_post_g[l]))
        wl = w_in[l]
        n_m = 4 * M_WIDTH
        gates = wl[:, n_m:n_m + 2 * M_HEADS]
        aq_akv = wl[:, n_m + 2 * M_HEADS:n_m + 2 * M_HEADS + Q_LORA + KV_LORA]
        ar = wl[:, n_m + 2 * M_HEADS + Q_LORA + KV_LORA:]
        zc = lambda w: jnp.zeros((D_MODEL, w), F32)
        aux_cols = jnp.concatenate([gates, zc(AUX_ROPE_LANE - 2 * M_HEADS), ar,
                                    zc(LANES - AUX_ROPE_LANE - A_ROPE)], axis=1)
        w_in_p = jnp.concatenate([wl[:, :n_m], aq_akv, aux_cols], axis=1).astype(BF16)
        bias_row = jnp.concatenate([b_igate[l], b_fgate[l]]).reshape(2 * M_HEADS, 1)
        w_uq_p = _pad_heads(w_uq[l], (A_NOPE, A_ROPE), A_HEADS).astype(BF16)
        wkv = w_ukv[l].reshape(KV_LORA, A_HEADS, A_NOPE + A_V)
        w_ukv_k = _pad_heads(wkv[:, :, :A_NOPE].reshape(KV_LORA, A_HEADS * A_NOPE), (A_NOPE,),
                             A_HEADS).astype(BF16)
        w_ukv_vt = wkv[:, :, A_NOPE:].reshape(KV_LORA, A_WIDTH).T.astype(BF16)
        mix = (row(mix_pre_g[l]), w_in_p, bias_row, row(q_a_g[l]), w_uq_p,
               row(kv_a_g[l]), row(m_head_g[l]), w_out[l].astype(BF16), row(mix_post_g[l]))
        y_p, st_p = _layer(y_p, pos_p, ff1, mix, ff2, w_ukv_k, w_ukv_vt, None,
                           mlstm_block=256, attn_tiles=(1024, 1024))
        cache = (cache_kv_latent[l], cache_k_rope[l], state_C[l], state_n[l], state_m[l])
        y_s, st_s = _layer(y_s, pos_s, ff1, mix, ff2, w_ukv_k, w_ukv_vt, cache,
                           mlstm_block=x_sample.shape[1], attn_tiles=(x_sample.shape[1], 4096))
        new_p.append(st_p)
        new_s.append(st_s)
    outs_p = [jnp.stack(a) for a in zip(*new_p)]
    outs_s = [jnp.stack(a) for a in zip(*new_s)]
    return (y_p, y_s, *outs_p, *outs_s)
```

```python
import functools
import math

import numpy as np
import jax
import jax.numpy as jnp
from jax import lax
from jax.experimental import pallas as pl
from jax.experimental.pallas import tpu as pltpu

F32 = jnp.float32
BF16 = jnp.bfloat16

D_MODEL = 1024
CHUNK = 64
CHUNK_SHIFT = 6
EPS = 1e-6
M_HEADS = 4
M_HEAD_DIM = 128
M_WIDTH = M_HEADS * M_HEAD_DIM
A_HEADS = 8
A_NOPE = 64
A_ROPE = 32
A_V = 64
A_WIDTH = A_HEADS * A_V
Q_LORA = 384
KV_LORA = 256
ROPE_THETA = 10000.0
A_SCALE = (A_NOPE + A_ROPE) ** -0.5
D_FF = 2816

LANES = 128
SUBLANES = 8
A_KPAD = A_HEADS * LANES
A_PANEL = 512
A_PANEL_DIAG = 256
M_STREAMS = 1
FFN_TILE, FFN_SUB = 1024, 256
A_ONES = 16
COL_MQ, COL_MK, COL_MV, COL_MO = 0, M_WIDTH, 2 * M_WIDTH, 3 * M_WIDTH
COL_AQ = 4 * M_WIDTH
COL_AKV = COL_AQ + Q_LORA
COL_AUX = COL_AKV + KV_LORA
IN_PAD = COL_AUX + LANES
AUX_ROPE_LANE = A_NOPE
GST_A, GST_B, GST_CMAX, GST_ROWS = 0, SUBLANES, 2 * SUBLANES, 3 * SUBLANES
VMEM_LIMIT = 56 * 1024 * 1024
NT_DIMS = (((1,), (1,)), ((), ()))


def _rms(x, g):
    return x * lax.rsqrt(jnp.mean(x * x, axis=-1, keepdims=True) + EPS) * g


def _const_spec(shape):
    nd = len(shape)
    return pl.BlockSpec(shape, lambda *_: (0,) * nd, pipeline_mode=pl.Buffered(1))


def _params(sem):
    return pltpu.CompilerParams(dimension_semantics=sem, vmem_limit_bytes=VMEM_LIMIT)


def _row_tile(n, cap, mult=16):
    best = n
    for t in range(mult, min(n, cap) + 1, mult):
        if n % t == 0:
            best = t
    return best


def _ffn_kernel(x_ref, *refs, mixer):
    if mixer:
        hm_ref, at_ref, w_out, mix_g, *refs = refs
    pre_g, wg, wu, wd, post_g, o_ref = refs
    sub = min(FFN_SUB, x_ref.shape[0])
    for r in range(0, x_ref.shape[0], sub):
        rows = slice(r, r + sub)
        x = x_ref[rows, :]
        if mixer:
            merged = jnp.concatenate([hm_ref[rows, :], at_ref[rows, :]], axis=-1)
            x = x + _rms(jnp.dot(merged, w_out[...], preferred_element_type=F32), mix_g[...])
        h = _rms(x, pre_g[...]).astype(BF16)
        g = jnp.dot(h, wg[...], preferred_element_type=F32)
        u = jnp.dot(h, wu[...], preferred_element_type=F32)
        a = (g * jax.nn.sigmoid(g) * u).astype(BF16)
        y = jnp.dot(a, wd[...], preferred_element_type=F32)
        o_ref[rows, :] = x + 0.5 * _rms(y, post_g[...])


def _ffn(x, ff, mixer=None):
    n = x.shape[0]
    tm = _row_tile(n, FFN_TILE)
    tok = lambda w: pl.BlockSpec((tm, w), lambda i: (i, 0))
    mix_specs =[] if mixer is None else [tok(M_WIDTH), tok(A_WIDTH),
                                          _const_spec((M_WIDTH + A_WIDTH, D_MODEL)),
                                          _const_spec((1, D_MODEL))]
    return pl.pallas_call(
        functools.partial(_ffn_kernel, mixer=mixer is not None),
        grid=(n // tm,),
        in_specs=[tok(D_MODEL)] + mix_specs
        + [_const_spec((1, D_MODEL)), _const_spec((D_MODEL, D_FF)), _const_spec((D_MODEL, D_FF)),
           _const_spec((D_FF, D_MODEL)), _const_spec((1, D_MODEL))],
        out_specs=tok(D_MODEL),
        out_shape=jax.ShapeDtypeStruct((n, D_MODEL), F32),
        compiler_params=_params(("parallel",)),
        name="ffn",
    )(x, *(mixer or ()), *ff)


def _log_sigmoid(x):
    return jnp.minimum(x, 0.0) - jnp.log(1.0 + jnp.exp(-jnp.abs(x)))


def _lane_scan(x, op, fill):
    n = x.shape[1]
    idx = lax.broadcasted_iota(jnp.int32, x.shape, 1)
    k = 1
    while k < n:
        x = op(x, jnp.where(idx >= k, pltpu.roll(x, k, axis=1), fill))
        k *= 2
    return x


def _rope_lanes(x, cos, sin_up, sin_dn):
    half = A_ROPE // 2
    return (x * cos + pltpu.roll(x, half, axis=1) * sin_up
            + pltpu.roll(x, LANES - half, axis=1) * sin_dn)


def _proj_kernel(x_ref, pre_g, w_in, brow_ref, q_a_g, w_uq, kv_a_g, cos_ref, sup_ref, sdn_ref,
                 mq_ref, mk_ref, mvt_ref, mo_ref, aux_ref, gst_ref, q_ref, ckv_ref, kr_ref, *, block):
    h = _rms(x_ref[...], pre_g[...]).astype(BF16)
    cos, sup, sdn = cos_ref[...], sup_ref[...], sdn_ref[...]
    aux = _rope_lanes(jnp.dot(h, w_in[:, COL_AUX:], preferred_element_type=F32), cos, sup, sdn)
    aux_ref[...] = aux
    kr_ref[...] = aux[:, AUX_ROPE_LANE:AUX_ROPE_LANE + A_ROPE]
    pre = aux.T[:2 * M_HEADS, :] + brow_ref[...]
    for c in range(pre.shape[1] // block):
        cs = slice(c * block, (c + 1) * block)
        b_all = pltpu.roll(_lane_scan(_log_sigmoid(pre[:, cs]), jnp.add, 0.0), M_HEADS, axis=0)
        a_all = pre[:, cs] - b_all
        gst_ref[GST_A:GST_A + SUBLANES, cs] = a_all
        gst_ref[GST_B:GST_B + SUBLANES, cs] = b_all
        gst_ref[GST_CMAX:GST_CMAX + SUBLANES, cs] = _lane_scan(a_all, jnp.maximum, -jnp.inf)
    aq = jnp.dot(h, w_in[:, COL_AQ:COL_AKV], preferred_element_type=F32)
    q = jnp.dot(_rms(aq, q_a_g[...]).astype(BF16), w_uq[...], preferred_element_type=F32)
    q_scale = A_SCALE * math.log2(math.e)
    for hd in range(A_HEADS):
        sl = slice(hd * LANES, (hd + 1) * LANES)
        q_ref[:, sl] = (_rope_lanes(q[:, sl], cos, sup, sdn) * q_scale).astype(BF16)
    p = jnp.dot(h, w_in[:, :COL_AQ], preferred_element_type=F32)
    mq_ref[...] = p[:, COL_MQ:COL_MQ + M_WIDTH].astype(BF16)
    mk_ref[...] = (p[:, COL_MK:COL_MK + M_WIDTH] * (M_HEAD_DIM ** -0.5)).astype(BF16)
    mvt_ref[...] = p[:, COL_MV:COL_MV + M_WIDTH].T.astype(BF16)
    mo_ref[...] = p[:, COL_MO:COL_MO + M_WIDTH].astype(BF16)
    akv = jnp.dot(h, w_in[:, COL_AKV:COL_AUX], preferred_element_type=F32)
    ckv_ref[...] = _rms(akv, kv_a_g[...])


def _proj(x, tables, pre_g, w_in, bias_row, q_a_g, w_uq, kv_a_g, block):
    b, s, _ = x.shape
    tm = _row_tile(s, 512, LANES)
    tok = lambda w: pl.BlockSpec((None, tm, w), lambda bi, j: (bi, j, 0))
    tok_t = lambda w: pl.BlockSpec((None, w, tm), lambda bi, j: (bi, 0, j))
    tab = pl.BlockSpec((tm, LANES), lambda bi, j: (j, 0))
    out_shape = [jax.ShapeDtypeStruct((b, s, M_WIDTH), BF16),
                 jax.ShapeDtypeStruct((b, s, M_WIDTH), BF16),
                 jax.ShapeDtypeStruct((b, M_WIDTH, s), BF16),
                 jax.ShapeDtypeStruct((b, s, M_WIDTH), BF16),
                 jax.ShapeDtypeStruct((b, s, LANES), F32),
                 jax.ShapeDtypeStruct((b, GST_ROWS, s), F32),
                 jax.ShapeDtypeStruct((b, s, A_KPAD), BF16),
                 jax.ShapeDtypeStruct((b, s, KV_LORA), F32),
                 jax.ShapeDtypeStruct((b, s, A_ROPE), F32)]
    out_specs = [tok(M_WIDTH), tok(M_WIDTH), tok_t(M_WIDTH), tok(M_WIDTH), tok(LANES),
                 tok_t(GST_ROWS), tok(A_KPAD), tok(KV_LORA), tok(A_ROPE)]
    return pl.pallas_call(
        functools.partial(_proj_kernel, block=block),
        grid=(b, s // tm),
        in_specs=[tok(D_MODEL), _const_spec((1, D_MODEL)), _const_spec((D_MODEL, IN_PAD)),
                  _const_spec((2 * M_HEADS, 1)),
                  _const_spec((1, Q_LORA)), _const_spec((Q_LORA, A_KPAD)),
                  _const_spec((1, KV_LORA)), tab, tab, tab],
        out_specs=out_specs,
        out_shape=out_shape,
        compiler_params=_params(("parallel", "parallel")),
        name="proj",
    )(x, pre_g, w_in, bias_row, q_a_g, w_uq, kv_a_g, *tables)


def _expand_kernel(c_ref, aux_ref, wk, wv_t, k_ref, vt_ref):
    cb = c_ref[...].astype(BF16)
    aux = aux_ref[...]
    lane = lax.broadcasted_iota(jnp.int32, aux.shape, 1)
    k_rope = jnp.where(lane >= AUX_ROPE_LANE, aux, 0.0)
    kn = jnp.dot(cb, wk[...], preferred_element_type=F32)
    for hd in range(A_HEADS):
        sl = slice(hd * LANES, (hd + 1) * LANES)
        k_ref[:, sl] = (kn[:, sl] + k_rope).astype(BF16)
    vt_ref[...] = lax.dot_general(wv_t[...], cb, NT_DIMS,
                                  preferred_element_type=F32).astype(BF16)


def _expand(c, aux, wk, wv_t):
    b, t, _ = c.shape
    tm = _row_tile(t, 1024, LANES)
    tok = lambda w: pl.BlockSpec((None, tm, w), lambda bi, j: (bi, j, 0))
    return pl.pallas_call(
        _expand_kernel,
        grid=(b, t // tm),
        in_specs=[tok(KV_LORA), tok(LANES), _const_spec((KV_LORA, A_KPAD)),
                  _const_spec((A_WIDTH, KV_LORA))],
        out_specs=[tok(A_KPAD), pl.BlockSpec((None, A_WIDTH, tm), lambda bi, j: (bi, 0, j))],
        out_shape=[jax.ShapeDtypeStruct((b, t, A_KPAD), BF16),
                   jax.ShapeDtypeStruct((b, A_WIDTH, t), BF16)],
        compiler_params=_params(("parallel", "parallel")),
        name="expand",
    )(c, aux, wk, wv_t)


def _mlstm_kernel(mq_ref, mk_ref, mvt_ref, mo_ref, gst_ref, hg_ref,
                  c0_ref, n0_ref, m0_ref, hm_ref, c_ref, n_ref, m_ref):
    @pl.when(pl.program_id(1) == 0)
    def _():
        c_ref[...] = c0_ref[...]
        n_ref[...] = n0_ref[...]
        m_ref[...] = m0_ref[...]

    nb, ln = mq_ref.shape[0], mq_ref.shape[1]
    src = lax.broadcasted_iota(jnp.int32, (ln, ln), 0)
    tgt = lax.broadcasted_iota(jnp.int32, (ln, ln), 1)
    causal = src <= tgt

    def gate_stats(bi):
        a_all = gst_ref[bi, GST_A:GST_A + SUBLANES, :]
        b_all = gst_ref[bi, GST_B:GST_B + SUBLANES, :]
        m_all = m_ref[bi][:, :1]
        g_all = jnp.maximum(m_all, gst_ref[bi, GST_CMAX:GST_CMAX + SUBLANES, :])
        a_cols = jnp.concatenate([a_all, jnp.zeros((LANES - SUBLANES, ln), F32)], axis=0).T
        return a_all, b_all, m_all, g_all, a_cols

    stats = [gate_stats(bi) for bi in range(nb)]

    def outputs(bi, hd):
        _, b_all, m_all, g_all, a_cols = stats[bi]
        sl = slice(hd * M_HEAD_DIM, (hd + 1) * M_HEAD_DIM)
        q, k, vt = mq_ref[bi, :, sl], mk_ref[bi, :, sl], mvt_ref[bi, sl, :]
        m_prev = m_all[hd:hd + 1, :]
        b_row, g_row = b_all[hd:hd + 1, :], g_all[hd:hd + 1, :]
        d = jnp.where(causal, jnp.exp(a_cols[:, hd:hd + 1] - g_row), 0.0)
        s = lax.dot_general(k, q, NT_DIMS, preferred_element_type=F32) * d
        w_carry = jnp.exp(m_prev - g_row)
        cq = lax.dot_general(c_ref[bi, hd].astype(BF16), q, NT_DIMS, preferred_element_type=F32)
        num = jnp.dot(vt, s.astype(BF16), preferred_element_type=F32) + w_carry * cq
        nq = lax.dot_general(n_ref[bi, hd].astype(BF16), q, NT_DIMS,
                             preferred_element_type=F32)[:1, :]
        den = jnp.sum(s, axis=0, keepdims=True) + w_carry * nq
        den = jnp.maximum(jnp.abs(den), jnp.exp(-(b_row + g_row)))
        hh = num / den
        return hh * lax.rsqrt(jnp.mean(hh * hh, axis=0, keepdims=True) + EPS)

    def finish(bi, hd, hh):
        _, b_all, m_all, g_all, a_cols = stats[bi]
        sl = slice(hd * M_HEAD_DIM, (hd + 1) * M_HEAD_DIM)
        k, vt = mk_ref[bi, :, sl], mvt_ref[bi, sl, :]
        a_all = stats[bi][0]
        gate = jax.nn.sigmoid(mo_ref[bi, :, sl].astype(F32))
        hm_ref[bi, :, sl] = (hh.T * hg_ref[:, sl] * gate).astype(BF16)
        m_prev = m_all[hd:hd + 1, :]
        g_last = g_all[hd:hd + 1, ln - 1:ln]
        w_state = jnp.exp(m_prev - g_last)
        w_row = jnp.exp(a_all[hd:hd + 1, :] - g_last)
        vw = (vt.astype(F32) * w_row).astype(BF16)
        c_ref[bi, hd] = w_state * c_ref[bi, hd] + jnp.dot(vw, k, preferred_element_type=F32)
        w_rows = jnp.broadcast_to(w_row, (SUBLANES, ln)).astype(BF16)
        n_ref[bi, hd] = w_state * n_ref[bi, hd] + jnp.dot(w_rows, k, preferred_element_type=F32)
        m_ref[bi, hd:hd + 1, :] = jnp.broadcast_to(b_all[hd:hd + 1, ln - 1:ln] + g_last, (1, LANES))

    units = [(bi, hd) for hd in range(M_HEADS) for bi in range(nb)]
    pending = {}
    for i in range(len(units) + nb):
        if i < len(units):
            pending[i] = outputs(*units[i])
        if 0 <= i - nb < len(units):
            finish(*units[i - nb], pending.pop(i - nb))


def _mlstm(mq, mk, mv, mo, gst, head_g, c0, n0, m0, block):
    b, s, _ = mq.shape
    nb = M_STREAMS if b % M_STREAMS == 0 else 1
    tok = lambda w: pl.BlockSpec((nb, block, w), lambda bi, j: (bi, j, 0))
    tok_t = lambda w: pl.BlockSpec((nb, w, block), lambda bi, j: (bi, 0, j))
    st_c = pl.BlockSpec((nb, M_HEADS, M_HEAD_DIM, M_HEAD_DIM), lambda bi, j: (bi, 0, 0, 0))
    st_n = pl.BlockSpec((nb, M_HEADS, SUBLANES, M_HEAD_DIM), lambda bi, j: (bi, 0, 0, 0))
    st_m = pl.BlockSpec((nb, SUBLANES, LANES), lambda bi, j: (bi, 0, 0))
    return pl.pallas_call(
        _mlstm_kernel,
        grid=(b // nb, s // block),
        in_specs=[tok(M_WIDTH), tok(M_WIDTH), tok_t(M_WIDTH), tok(M_WIDTH), tok_t(GST_ROWS),
                  _const_spec((1, M_WIDTH)), st_c, st_n, st_m],
        out_specs=[tok(M_WIDTH), st_c, st_n, st_m],
        out_shape=[jax.ShapeDtypeStruct((b, s, M_WIDTH), BF16),
                   jax.ShapeDtypeStruct((b, M_HEADS, M_HEAD_DIM, M_HEAD_DIM), F32),
                   jax.ShapeDtypeStruct((b, M_HEADS, SUBLANES, M_HEAD_DIM), F32),
                   jax.ShapeDtypeStruct((b, SUBLANES, LANES), F32)],
        compiler_params=_params(("parallel", "arbitrary")),
        name="mlstm",
    )(mq, mk, mv, mo, gst, head_g, c0, n0, m0)


def _attn_kernel(qi_ref, ki_ref, last_ref, diag_ref, q_ref, k_ref, vt_ref, o_ref,
                 m_scr, acc_scr, *, has_diag):
    t = pl.program_id(1)
    ki = ki_ref[t]
    tq, tk = q_ref.shape[0], k_ref.shape[0]

    @pl.when(ki == 0)
    def _():
        m_scr[...] = jnp.full(m_scr.shape, -jnp.inf, F32)
        acc_scr[...] = jnp.zeros(acc_scr.shape, F32)

    def step(diag):
        panel = min(A_PANEL_DIAG if diag else A_PANEL, tq)
        if diag:
            k_chunk = lax.broadcasted_iota(jnp.int32, (panel, panel), 0) >> CHUNK_SHIFT
            q_chunk = lax.broadcasted_iota(jnp.int32, (panel, panel), 1) >> CHUNK_SHIFT
            visible = q_chunk >= k_chunk
        ones = jnp.ones((A_ONES, tk), BF16)

        def scores(hd, c):
            sl, cs = slice(hd * LANES, (hd + 1) * LANES), slice(c, c + panel)
            n_keys = c + panel if diag else tk
            s = lax.dot_general(k_ref[:n_keys, sl], q_ref[cs, sl], NT_DIMS,
                                preferred_element_type=F32)
            if diag:
                square = jnp.where(visible, s[c:, :], -jnp.inf)
                s = jnp.concatenate([s[:c, :], square], axis=0) if c else square
            m_prev = m_scr[hd, :, cs]
            m_new = jnp.maximum(m_prev, jnp.max(s, axis=0, keepdims=True))
            m_scr[hd, :, cs] = m_new
            return s, m_new, jnp.exp2(m_prev - m_new)

        def probs(s, m_new):
            return jnp.exp2(s - m_new).astype(BF16)

        def accumulate(hd, c, p, alpha):
            n_keys, cs = p.shape[0], slice(c, c + panel)
            vt = jnp.concatenate([vt_ref[hd * A_V:(hd + 1) * A_V, :n_keys], ones[:, :n_keys]], axis=0)
            acc_scr[hd, :, cs] = alpha * acc_scr[hd, :, cs] + jnp.dot(vt, p, preferred_element_type=F32)

        units = [(hd, c) for hd in range(A_HEADS) for c in range(0, tq, panel)]
        st_s, st_p = {}, {}
        for i in range(len(units) + 2):
            if i < len(units):
                st_s[i] = scores(*units[i])
            if 0 <= i - 1 < len(units):
                s, m_new, alpha = st_s.pop(i - 1)
                st_p[i - 1] = (probs(s, m_new), alpha)
            if 0 <= i - 2 < len(units):
                accumulate(*units[i - 2], *st_p.pop(i - 2))

    if has_diag:
        pl.when(diag_ref[t] == 1)(lambda: step(True))
    pl.when(diag_ref[t] == 0)(lambda: step(False))

    @pl.when(last_ref[t] == 1)
    def _():
        out_t = jnp.concatenate([acc_scr[hd][:A_V, :] / acc_scr[hd][A_V:A_V + 1, :]
                                 for hd in range(A_HEADS)], axis=0)
        o_ref[...] = out_t.T.astype(BF16)


def _attn(q, k, vt, q_off, tq, tk):
    b, sq, _ = q.shape
    sk = k.shape[1]
    nq, nk = sq // tq, sk // tk
    qi, ki, last, diag = [], [], [], []
    for i in range(nq):
        q_first, q_last = q_off + i * tq, q_off + i * tq + tq - 1
        j_end = min(nk - 1, (((q_last >> CHUNK_SHIFT) + 1) * CHUNK - 1) // tk)
        for j in range(j_end + 1):
            needs_mask = ((j * tk + tk - 1) >> CHUNK_SHIFT) > (q_first >> CHUNK_SHIFT)
            if needs_mask and not (tq == tk and q_first == j * tk and tk % min(A_PANEL_DIAG, tq) == 0):
                raise NotImplementedError("masked attention tiles must be diagonal and panel aligned")
            qi.append(i)
            ki.append(j)
            last.append(int(j == j_end))
            diag.append(int(needs_mask))
    arrs = [jnp.asarray(np.asarray(a, np.int32)) for a in (qi, ki, last, diag)]
    grid_spec = pltpu.PrefetchScalarGridSpec(
        num_scalar_prefetch=4,
        grid=(b, len(qi)),
        in_specs=[pl.BlockSpec((None, tq, A_KPAD), lambda bi, t, qi, ki, la, dg: (bi, qi[t], 0)),
                  pl.BlockSpec((None, tk, A_KPAD), lambda bi, t, qi, ki, la, dg: (bi, ki[t], 0)),
                  pl.BlockSpec((None, A_WIDTH, tk), lambda bi, t, qi, ki, la, dg: (bi, 0, ki[t]))],
        out_specs=pl.BlockSpec((None, tq, A_WIDTH), lambda bi, t, qi, ki, la, dg: (bi, qi[t], 0)),
        scratch_shapes=[pltpu.VMEM((A_HEADS, 1, tq), F32),
                        pltpu.VMEM((A_HEADS, A_V + A_ONES, tq), F32)],
    )
    return pl.pallas_call(
        functools.partial(_attn_kernel, has_diag=any(diag)),
        grid_spec=grid_spec,
        out_shape=jax.ShapeDtypeStruct((b, sq, A_WIDTH), BF16),
        compiler_params=_params(("parallel", "arbitrary")),
        name="attn",
    )(*arrs, q, k, vt)


def _rope_tables(pos):
    half = A_ROPE // 2
    n = pos.shape[0]
    freq = ROPE_THETA ** (-jnp.arange(half, dtype=F32) / half)
    ang = pos.astype(F32)[:, None] * freq[None, :]
    if (n * half) % LANES == 0:
        ang = ang.reshape(n * half // LANES, LANES)
    cos, sin = lax.optimization_barrier((jnp.cos(ang), jnp.sin(ang)))
    cos, sin = cos.reshape(n, half), sin.reshape(n, half)
    ones = jnp.ones((n, A_NOPE), F32)
    z = lambda w: jnp.zeros((n, w), F32)
    tail = LANES - A_NOPE - A_ROPE
    cos_t = jnp.concatenate([ones, cos, cos, z(tail)], axis=1)
    sin_up = jnp.concatenate([z(A_NOPE + half), sin, z(tail)], axis=1)
    sin_dn = jnp.concatenate([z(A_NOPE), -sin, z(half + tail)], axis=1)
    return cos_t, sin_up, sin_dn


def _pad_heads(w, widths, n_heads):
    kdim = w.shape[0]
    w = w.reshape(kdim, n_heads, sum(widths))
    w = jnp.pad(w, ((0, 0), (0, 0), (0, LANES - sum(widths))))
    return w.reshape(kdim, n_heads * LANES)


def _layer(x, pos, ff1, mix, ff2, w_ukv_k, w_ukv_vt, cache, mlstm_block, attn_tiles):
    b, s, _ = x.shape
    mix_pre_g, w_in_p, bias_row, q_a_g, w_uq_p, kv_a_g, head_g, w_out, mix_post_g = mix
    x1 = _ffn(x.reshape(b * s, D_MODEL), ff1).reshape(b, s, D_MODEL)
    mq, mk, mv, mo, aux, gst, q, c_kv, k_rope = _proj(x1, _rope_tables(pos), mix_pre_g, w_in_p,
                                                      bias_row, q_a_g, w_uq_p, kv_a_g, mlstm_block)
    if cache is None:
        c_all, aux_all, q_off = c_kv, aux, 0
        c0 = jnp.zeros((b, M_HEADS, M_HEAD_DIM, M_HEAD_DIM), F32)
        n0 = jnp.zeros((b, M_HEADS, SUBLANES, M_HEAD_DIM), F32)
        m0 = jnp.zeros((b, SUBLANES, LANES), F32)
    else:
        cache_kv, cache_kr, c0, n0, m0 = cache
        q_off = cache_kv.shape[1]
        c_all = jnp.concatenate([cache_kv, c_kv], axis=1)
        kr_pad = jnp.pad(cache_kr, ((0, 0), (0, 0), (AUX_ROPE_LANE, LANES - AUX_ROPE_LANE - A_ROPE)))
        aux_all = jnp.concatenate([kr_pad, aux], axis=1)
        n0 = jnp.pad(n0[:, :, None, :], ((0, 0), (0, 0), (0, SUBLANES - 1), (0, 0)))
        m0 = jnp.broadcast_to(jnp.pad(m0, ((0, 0), (0, SUBLANES - M_HEADS)))[:, :, None],
                              (b, SUBLANES, LANES))
    k, vt = _expand(c_all, aux_all, w_ukv_k, w_ukv_vt)
    hm, c_new, n_new, m_new = _mlstm(mq, mk, mv, mo, gst, head_g, c0, n0, m0, mlstm_block)
    tq, tk = attn_tiles
    at = _attn(q, k, vt, q_off, tq, min(tk, k.shape[1]))
    n_tok = b * s
    mixer = (hm.reshape(n_tok, M_WIDTH), at.reshape(n_tok, A_WIDTH), w_out, mix_post_g)
    y = _ffn(x1.reshape(n_tok, D_MODEL), ff2, mixer).reshape(b, s, D_MODEL)
    state = (c_kv, k_rope, c_new, n_new[:, :, 0, :], m_new[:, :M_HEADS, 0])
    return y, state


def kernel(x_prompt, x_sample, cache_kv_latent, cache_k_rope, state_C, state_n, state_m, ff1_pre_g, ff1_w_gate, ff1_w_up, ff1_w_down, ff1_post_g, mix_pre_g, w_in, b_igate, b_fgate, q_a_g, w_uq, kv_a_g, w_ukv, m_head_g, w_out, mix_post_g, ff2_pre_g, ff2_w_gate, ff2_w_up, ff2_w_down, ff2_post_g):
    depth = w_in.shape[0]
    y_p, y_s = x_prompt, x_sample
    pos_p = jnp.arange(x_prompt.shape[1])
    pos_s = cache_kv_latent.shape[2] + jnp.arange(x_sample.shape[1])
    new_p, new_s = [], []
    row = lambda g: g.reshape(1, -1)
    for l in range(depth):
        ff1 = (row(ff1_pre_g[l]), ff1_w_gate[l].astype(BF16), ff1_w_up[l].astype(BF16),
               ff1_w_down[l].astype(BF16), row(ff1_post_g[l]))
        ff2 = (row(ff2_pre_g[l]), ff2_w_gate[l].astype(BF16), ff2_w_up[l].astype(BF16),
               ff2_w_down[l].astype(BF16), row(ff2_post_g[l]))
        wl = w_in[l]
        n_m = 4 * M_WIDTH
        gates = wl[:, n_m:n_m + 2 * M_HEADS]
        aq_akv = wl[:, n_m + 2 * M_HEADS:n_m + 2 * M_HEADS + Q_LORA + KV_LORA]
        ar = wl[:, n_m + 2 * M_HEADS + Q_LORA + KV_LORA:]
        zc = lambda w: jnp.zeros((D_MODEL, w), F32)
        aux_cols = jnp.concatenate([gates, zc(AUX_ROPE_LANE - 2 * M_HEADS), ar,
                                    zc(LANES - AUX_ROPE_LANE - A_ROPE)], axis=1)
        w_in_p = jnp.concatenate([wl[:, :n_m], aq_akv, aux_cols], axis=1).astype(BF16)
        bias_row = jnp.concatenate([b_igate[l], b_fgate[l]]).reshape(2 * M_HEADS, 1)
        w_uq_p = _pad_heads(w_uq[l], (A_NOPE, A_ROPE), A_HEADS).astype(BF16)
        wkv = w_ukv[l].reshape(KV_LORA, A_HEADS, A_NOPE + A_V)
        w_ukv_k = _pad_heads(wkv[:, :, :A_NOPE].reshape(KV_LORA, A_HEADS * A_NOPE), (A_NOPE,),
                             A_HEADS).astype(BF16)
        w_ukv_vt = wkv[:, :, A_NOPE:].reshape(KV_LORA, A_WIDTH).T.astype(BF16)
        mix = (row(mix_pre_g[l]), w_in_p, bias_row, row(q_a_g[l]), w_uq_p,
               row(kv_a_g[l]), row(m_head_g[l]), w_out[l].astype(BF16), row(mix_post_g[l]))
        y_p, st_p = _layer(y_p, pos_p, ff1, mix, ff2, w_ukv_k, w_ukv_vt, None,
                           mlstm_block=256, attn_tiles=(1024, 1024))
        cache = (cache_kv_latent[l], cache_k_rope[l], state_C[l], state_n[l], state_m[l])
        y_s, st_s = _layer(y_s, pos_s, ff1, mix, ff2, w_ukv_k, w_ukv_vt, cache,
                           mlstm_block=x_sample.shape[1], attn_tiles=(x_sample.shape[1], 4096))
        new_p.append(st_p)
        new_s.append(st_s)
    outs_p = [jnp.stack(a) for a in zip(*new_p)]
    outs_s = [jnp.stack(a) for a in zip(*new_s)]
    return (y_p, y_s, *outs_p, *outs_s)
```

```python
import functools
import math

import numpy as np
import jax
import jax.numpy as jnp
from jax import lax
from jax.experimental import pallas as pl
from jax.experimental.pallas import tpu as pltpu

F32 = jnp.float32
BF16 = jnp.bfloat16

D_MODEL = 1024
CHUNK = 64
CHUNK_SHIFT = 6
EPS = 1e-6
M_HEADS = 4
M_HEAD_DIM = 128
M_WIDTH = M_HEADS * M_HEAD_DIM
A_HEADS = 8
A_NOPE = 64
A_ROPE = 32
A_V = 64
A_WIDTH = A_HEADS * A_V
Q_LORA = 384
KV_LORA = 256
ROPE_THETA = 10000.0
A_SCALE = (A_NOPE + A_ROPE) ** -0.5
D_FF = 2816

LANES = 128
SUBLANES = 8
A_KPAD = A_HEADS * LANES
A_PANEL = 512
A_PANEL_DIAG = 256
M_STREAMS = 1
FFN_TILE, FFN_SUB = 1024, 256
A_ONES = 16
COL_MQ, COL_MK, COL_MV, COL_MO = 0, M_WIDTH, 2 * M_WIDTH, 3 * M_WIDTH
COL_AQ = 4 * M_WIDTH
COL_AKV = COL_AQ + Q_LORA
COL_AUX = COL_AKV + KV_LORA
IN_PAD = COL_AUX + LANES
AUX_ROPE_LANE = A_NOPE
GST_A, GST_B, GST_CMAX, GST_ROWS = 0, SUBLANES, 2 * SUBLANES, 3 * SUBLANES
VMEM_LIMIT = 56 * 1024 * 1024
NT_DIMS = (((1,), (1,)), ((), ()))


def _rms(x, g):
    return x * lax.rsqrt(jnp.mean(x * x, axis=-1, keepdims=True) + EPS) * g


def _const_spec(shape):
    nd = len(shape)
    return pl.BlockSpec(shape, lambda *_: (0,) * nd, pipeline_mode=pl.Buffered(1))


def _params(sem):
    return pltpu.CompilerParams(dimension_semantics=sem, vmem_limit_bytes=VMEM_LIMIT)


def _row_tile(n, cap, mult=16):
    best = n
    for t in range(mult, min(n, cap) + 1, mult):
        if n % t == 0:
            best = t
    return best


def _ffn_kernel(x_ref, *refs, mixer):
    if mixer:
        hm_ref, at_ref, w_out, mix_g, *refs = refs
    pre_g, wg, wu, wd, post_g, o_ref = refs
    sub = min(FFN_SUB, x_ref.shape[0])
    for r in range(0, x_ref.shape[0], sub):
        rows = slice(r, r + sub)
        x = x_ref[rows, :]
        if mixer:
            merged = jnp.concatenate([hm_ref[rows, :], at_ref[rows, :]], axis=-1)
            x = x + _rms(jnp.dot(merged, w_out[...], preferred_element_type=F32), mix_g[...])
        h = _rms(x, pre_g[...]).astype(BF16)
        g = jnp.dot(h, wg[...], preferred_element_type=F32)
        u = jnp.dot(h, wu[...], preferred_element_type=F32)
        a = (g * jax.nn.sigmoid(g) * u).astype(BF16)
        y = jnp.dot(a, wd[...], preferred_element_type=F32)
        o_ref[rows, :] = x + 0.5 * _rms(y, post_g[...])


def _ffn(x, ff, mixer=None):
    n = x.shape[0]
    tm = _row_tile(n, FFN_TILE)
    tok = lambda w: pl.BlockSpec((tm, w), lambda i: (i, 0))
    mix_specs =[] if mixer is None else [tok(M_WIDTH), tok(A_WIDTH),
                                          _const_spec((M_WIDTH + A_WIDTH, D_MODEL)),
                                          _const_spec((1, D_MODEL))]
    return pl.pallas_call(
        functools.partial(_ffn_kernel, mixer=mixer is not None),
        grid=(n // tm,),
        in_specs=[tok(D_MODEL)] + mix_specs
        + [_const_spec((1, D_MODEL)), _const_spec((D_MODEL, D_FF)), _const_spec((D_MODEL, D_FF)),
           _const_spec((D_FF, D_MODEL)), _const_spec((1, D_MODEL))],
        out_specs=tok(D_MODEL),
        out_shape=jax.ShapeDtypeStruct((n, D_MODEL), F32),
        compiler_params=_params(("parallel",)),
        name="ffn",
    )(x, *(mixer or ()), *ff)


def _log_sigmoid(x):
    return jnp.minimum(x, 0.0) - jnp.log(1.0 + jnp.exp(-jnp.abs(x)))


def _lane_scan(x, op, fill):
    n = x.shape[1]
    idx = lax.broadcasted_iota(jnp.int32, x.shape, 1)
    k = 1
    while k < n:
        x = op(x, jnp.where(idx >= k, pltpu.roll(x, k, axis=1), fill))
        k *= 2
    return x


def _rope_lanes(x, cos, sin_up, sin_dn):
    half = A_ROPE // 2
    return (x * cos + pltpu.roll(x, half, axis=1) * sin_up
            + pltpu.roll(x, LANES - half, axis=1) * sin_dn)


def _proj_kernel(x_ref, pre_g, w_in, brow_ref, q_a_g, w_uq, kv_a_g, cos_ref, sup_ref, sdn_ref,
                 mq_ref, mk_ref, mvt_ref, mo_ref, aux_ref, gst_ref, q_ref, ckv_ref, kr_ref, *, block):
    h = _rms(x_ref[...], pre_g[...]).astype(BF16)
    cos, sup, sdn = cos_ref[...], sup_ref[...], sdn_ref[...]
    aux = _rope_lanes(jnp.dot(h, w_in[:, COL_AUX:], preferred_element_type=F32), cos, sup, sdn)
    aux_ref[...] = aux
    kr_ref[...] = aux[:, AUX_ROPE_LANE:AUX_ROPE_LANE + A_ROPE]
    pre = aux.T[:2 * M_HEADS, :] + brow_ref[...]
    for c in range(pre.shape[1] // block):
        cs = slice(c * block, (c + 1) * block)
        b_all = pltpu.roll(_lane_scan(_log_sigmoid(pre[:, cs]), jnp.add, 0.0), M_HEADS, axis=0)
        a_all = pre[:, cs] - b_all
        gst_ref[GST_A:GST_A + SUBLANES, cs] = a_all
        gst_ref[GST_B:GST_B + SUBLANES, cs] = b_all
        gst_ref[GST_CMAX:GST_CMAX + SUBLANES, cs] = _lane_scan(a_all, jnp.maximum, -jnp.inf)
    aq = jnp.dot(h, w_in[:, COL_AQ:COL_AKV], preferred_element_type=F32)
    q = jnp.dot(_rms(aq, q_a_g[...]).astype(BF16), w_uq[...], preferred_element_type=F32)
    q_scale = A_SCALE * math.log2(math.e)
    for hd in range(A_HEADS):
        sl = slice(hd * LANES, (hd + 1) * LANES)
        q_ref[:, sl] = (_rope_lanes(q[:, sl], cos, sup, sdn) * q_scale).astype(BF16)
    p = jnp.dot(h, w_in[:, :COL_AQ], preferred_element_type=F32)
    mq_ref[...] = p[:, COL_MQ:COL_MQ + M_WIDTH].astype(BF16)
    mk_ref[...] = (p[:, COL_MK:COL_MK + M_WIDTH] * (M_HEAD_DIM ** -0.5)).astype(BF16)
    mvt_ref[...] = p[:, COL_MV:COL_MV + M_WIDTH].T.astype(BF16)
    mo_ref[...] = p[:, COL_MO:COL_MO + M_WIDTH].astype(BF16)
    akv = jnp.dot(h, w_in[:, COL_AKV:COL_AUX], preferred_element_type=F32)
    ckv_ref[...] = _rms(akv, kv_a_g[...])


def _proj(x, tables, pre_g, w_in, bias_row, q_a_g, w_uq, kv_a_g, block):
    b, s, _ = x.shape
    tm = _row_tile(s, 512, LANES)
    tok = lambda w: pl.BlockSpec((None, tm, w), lambda bi, j: (bi, j, 0))
    tok_t = lambda w: pl.BlockSpec((None, w, tm), lambda bi, j: (bi, 0, j))
    tab = pl.BlockSpec((tm, LANES), lambda bi, j: (j, 0))
    out_shape = [jax.ShapeDtypeStruct((b, s, M_WIDTH), BF16),
                 jax.ShapeDtypeStruct((b, s, M_WIDTH), BF16),
                 jax.ShapeDtypeStruct((b, M_WIDTH, s), BF16),
                 jax.ShapeDtypeStruct((b, s, M_WIDTH), BF16),
                 jax.ShapeDtypeStruct((b, s, LANES), F32),
                 jax.ShapeDtypeStruct((b, GST_ROWS, s), F32),
                 jax.ShapeDtypeStruct((b, s, A_KPAD), BF16),
                 jax.ShapeDtypeStruct((b, s, KV_LORA), F32),
                 jax.ShapeDtypeStruct((b, s, A_ROPE), F32)]
    out_specs = [tok(M_WIDTH), tok(M_WIDTH), tok_t(M_WIDTH), tok(M_WIDTH), tok(LANES),
                 tok_t(GST_ROWS), tok(A_KPAD), tok(KV_LORA), tok(A_ROPE)]
    return pl.pallas_call(
        functools.partial(_proj_kernel, block=block),
        grid=(b, s // tm),
        in_specs=[tok(D_MODEL), _const_spec((1, D_MODEL)), _const_spec((D_MODEL, IN_PAD)),
                  _const_spec((2 * M_HEADS, 1)),
                  _const_spec((1, Q_LORA)), _const_spec((Q_LORA, A_KPAD)),
                  _const_spec((1, KV_LORA)), tab, tab, tab],
        out_specs=out_specs,
        out_shape=out_shape,
        compiler_params=_params(("parallel", "parallel")),
        name="proj",
    )(x, pre_g, w_in, bias_row, q_a_g, w_uq, kv_a_g, *tables)


def _expand_kernel(c_ref, aux_ref, wk, wv_t, k_ref, vt_ref):
    cb = c_ref[...].astype(BF16)
    aux = aux_ref[...]
    lane = lax.broadcasted_iota(jnp.int32, aux.shape, 1)
    k_rope = jnp.where(lane >= AUX_ROPE_LANE, aux, 0.0)
    kn = jnp.dot(cb, wk[...], preferred_element_type=F32)
    for hd in range(A_HEADS):
        sl = slice(hd * LANES, (hd + 1) * LANES)
        k_ref[:, sl] = (kn[:, sl] + k_rope).astype(BF16)
    vt_ref[...] = lax.dot_general(wv_t[...], cb, NT_DIMS,
                                  preferred_element_type=F32).astype(BF16)


def _expand(c, aux, wk, wv_t):
    b, t, _ = c.shape
    tm = _row_tile(t, 1024, LANES)
    tok = lambda w: pl.BlockSpec((None, tm, w), lambda bi, j: (bi, j, 0))
    return pl.pallas_call(
        _expand_kernel,
        grid=(b, t // tm),
        in_specs=[tok(KV_LORA), tok(LANES), _const_spec((KV_LORA, A_KPAD)),
                  _const_spec((A_WIDTH, KV_LORA))],
        out_specs=[tok(A_KPAD), pl.BlockSpec((None, A_WIDTH, tm), lambda bi, j: (bi, 0, j))],
        out_shape=[jax.ShapeDtypeStruct((b, t, A_KPAD), BF16),
                   jax.ShapeDtypeStruct((b, A_WIDTH, t), BF16)],
        compiler_params=_params(("parallel", "parallel")),
        name="expand",
    )(c, aux, wk, wv_t)


def _mlstm_kernel(mq_ref, mk_ref, mvt_ref, mo_ref, gst_ref, hg_ref,
                  c0_ref, n0_ref, m0_ref, hm_ref, c_ref, n_ref, m_ref):
    @pl.when(pl.program_id(1) == 0)
    def _():
        c_ref[...] = c0_ref[...]
        n_ref[...] = n0_ref[...]
        m_ref[...] = m0_ref[...]

    nb, ln = mq_ref.shape[0], mq_ref.shape[1]
    src = lax.broadcasted_iota(jnp.int32, (ln, ln), 0)
    tgt = lax.broadcasted_iota(jnp.int32, (ln, ln), 1)
    causal = src <= tgt

    def gate_stats(bi):
        a_all = gst_ref[bi, GST_A:GST_A + SUBLANES, :]
        b_all = gst_ref[bi, GST_B:GST_B + SUBLANES, :]
        m_all = m_ref[bi][:, :1]
        g_all = jnp.maximum(m_all, gst_ref[bi, GST_CMAX:GST_CMAX + SUBLANES, :])
        a_cols = jnp.concatenate([a_all, jnp.zeros((LANES - SUBLANES, ln), F32)], axis=0).T
        return a_all, b_all, m_all, g_all, a_cols

    stats = [gate_stats(bi) for bi in range(nb)]

    def outputs(bi, hd):
        _, b_all, m_all, g_all, a_cols = stats[bi]
        sl = slice(hd * M_HEAD_DIM, (hd + 1) * M_HEAD_DIM)
        q, k, vt = mq_ref[bi, :, sl], mk_ref[bi, :, sl], mvt_ref[bi, sl, :]
        m_prev = m_all[hd:hd + 1, :]
        b_row, g_row = b_all[hd:hd + 1, :], g_all[hd:hd + 1, :]
        d = jnp.where(causal, jnp.exp(a_cols[:, hd:hd + 1] - g_row), 0.0)
        s = lax.dot_general(k, q, NT_DIMS, preferred_element_type=F32) * d
        w_carry = jnp.exp(m_prev - g_row)
        cq = lax.dot_general(c_ref[bi, hd].astype(BF16), q, NT_DIMS, preferred_element_type=F32)
        num = jnp.dot(vt, s.astype(BF16), preferred_element_type=F32) + w_carry * cq
        nq = lax.dot_general(n_ref[bi, hd].astype(BF16), q, NT_DIMS,
                             preferred_element_type=F32)[:1, :]
        den = jnp.sum(s, axis=0, keepdims=True) + w_carry * nq
        den = jnp.maximum(jnp.abs(den), jnp.exp(-(b_row + g_row)))
        hh = num / den
        return hh * lax.rsqrt(jnp.mean(hh * hh, axis=0, keepdims=True) + EPS)

    def finish(bi, hd, hh):
        _, b_all, m_all, g_all, a_cols = stats[bi]
        sl = slice(hd * M_HEAD_DIM, (hd + 1) * M_HEAD_DIM)
        k, vt = mk_ref[bi, :, sl], mvt_ref[bi, sl, :]
        a_all = stats[bi][0]
        gate = jax.nn.sigmoid(mo_ref[bi, :, sl].astype(F32))
        hm_ref[bi, :, sl] = (hh.T * hg_ref[:, sl] * gate).astype(BF16)
        m_prev = m_all[hd:hd + 1, :]
        g_last = g_all[hd:hd + 1, ln - 1:ln]
        w_state = jnp.exp(m_prev - g_last)
        w_row = jnp.exp(a_all[hd:hd + 1, :] - g_last)
        vw = (vt.astype(F32) * w_row).astype(BF16)
        c_ref[bi, hd] = w_state * c_ref[bi, hd] + jnp.dot(vw, k, preferred_element_type=F32)
        w_rows = jnp.broadcast_to(w_row, (SUBLANES, ln)).astype(BF16)
        n_ref[bi, hd] = w_state * n_ref[bi, hd] + jnp.dot(w_rows, k, preferred_element_type=F32)
        m_ref[bi, hd:hd + 1, :] = jnp.broadcast_to(b_all[hd:hd + 1, ln - 1:ln] + g_last, (1, LANES))

    units = [(bi, hd) for hd in range(M_HEADS) for bi in range(nb)]
    pending = {}
    for i in range(len(units) + nb):
        if i < len(units):
            pending[i] = outputs(*units[i])
        if 0 <= i - nb < len(units):
            finish(*units[i - nb], pending.pop(i - nb))


def _mlstm(mq, mk, mv, mo, gst, head_g, c0, n0, m0, block):
    b, s, _ = mq.shape
    nb = M_STREAMS if b % M_STREAMS == 0 else 1
    tok = lambda w: pl.BlockSpec((nb, block, w), lambda bi, j: (bi, j, 0))
    tok_t = lambda w: pl.BlockSpec((nb, w, block), lambda bi, j: (bi, 0, j))
    st_c = pl.BlockSpec((nb, M_HEADS, M_HEAD_DIM, M_HEAD_DIM), lambda bi, j: (bi, 0, 0, 0))
    st_n = pl.BlockSpec((nb, M_HEADS, SUBLANES, M_HEAD_DIM), lambda bi, j: (bi, 0, 0, 0))
    st_m = pl.BlockSpec((nb, SUBLANES, LANES), lambda bi, j: (bi, 0, 0))
    return pl.pallas_call(
        _mlstm_kernel,
        grid=(b // nb, s // block),
        in_specs=[tok(M_WIDTH), tok(M_WIDTH), tok_t(M_WIDTH), tok(M_WIDTH), tok_t(GST_ROWS),
                  _const_spec((1, M_WIDTH)), st_c, st_n, st_m],
        out_specs=[tok(M_WIDTH), st_c, st_n, st_m],
        out_shape=[jax.ShapeDtypeStruct((b, s, M_WIDTH), BF16),
                   jax.ShapeDtypeStruct((b, M_HEADS, M_HEAD_DIM, M_HEAD_DIM), F32),
                   jax.ShapeDtypeStruct((b, M_HEADS, SUBLANES, M_HEAD_DIM), F32),
                   jax.ShapeDtypeStruct((b, SUBLANES, LANES), F32)],
        compiler_params=_params(("parallel", "arbitrary")),
        name="mlstm",
    )(mq, mk, mv, mo, gst, head_g, c0, n0, m0)


def _attn_kernel(qi_ref, ki_ref, last_ref, diag_ref, q_ref, k_ref, vt_ref, o_ref,
                 m_scr, acc_scr, *, has_diag):
    t = pl.program_id(1)
    ki = ki_ref[t]
    tq, tk = q_ref.shape[0], k_ref.shape[0]

    @pl.when(ki == 0)
    def _():
        m_scr[...] = jnp.full(m_scr.shape, -jnp.inf, F32)
        acc_scr[...] = jnp.zeros(acc_scr.shape, F32)

    def step(diag):
        panel = min(A_PANEL_DIAG if diag else A_PANEL, tq)
        if diag:
            k_chunk = lax.broadcasted_iota(jnp.int32, (panel, panel), 0) >> CHUNK_SHIFT
            q_chunk = lax.broadcasted_iota(jnp.int32, (panel, panel), 1) >> CHUNK_SHIFT
            visible = q_chunk >= k_chunk
        ones = jnp.ones((A_ONES, tk), BF16)

        def scores(hd, c):
            sl, cs = slice(hd * LANES, (hd + 1) * LANES), slice(c, c + panel)
            n_keys = c + panel if diag else tk
            s = lax.dot_general(k_ref[:n_keys, sl], q_ref[cs, sl], NT_DIMS,
                                preferred_element_type=F32)
            if diag:
                square = jnp.where(visible, s[c:, :], -jnp.inf)
                s = jnp.concatenate([s[:c, :], square], axis=0) if c else square
            m_prev = m_scr[hd, :, cs]
            m_new = jnp.maximum(m_prev, jnp.max(s, axis=0, keepdims=True))
            m_scr[hd, :, cs] = m_new
            return s, m_new, jnp.exp2(m_prev - m_new)

        def probs(s, m_new):
            return jnp.exp2(s - m_new).astype(BF16)

        def accumulate(hd, c, p, alpha):
            n_keys, cs = p.shape[0], slice(c, c + panel)
            vt = jnp.concatenate([vt_ref[hd * A_V:(hd + 1) * A_V, :n_keys], ones[:, :n_keys]], axis=0)
            acc_scr[hd, :, cs] = alpha * acc_scr[hd, :, cs] + jnp.dot(vt, p, preferred_element_type=F32)

        units = [(hd, c) for hd in range(A_HEADS) for c in range(0, tq, panel)]
        st_s, st_p = {}, {}
        for i in range(len(units) + 2):
            if i < len(units):
                st_s[i] = scores(*units[i])
            if 0 <= i - 1 < len(units):
                s, m_new, alpha = st_s.pop(i - 1)
                st_p[i - 1] = (probs(s, m_new), alpha)
            if 0 <= i - 2 < len(units):
                accumulate(*units[i - 2], *st_p.pop(i - 2))

    if has_diag:
        pl.when(diag_ref[t] == 1)(lambda: step(True))
    pl.when(diag_ref[t] == 0)(lambda: step(False))

    @pl.when(last_ref[t] == 1)
    def _():
        out_t = jnp.concatenate([acc_scr[hd][:A_V, :] / acc_scr[hd][A_V:A_V + 1, :]
                                 for hd in range(A_HEADS)], axis=0)
        o_ref[...] = out_t.T.astype(BF16)


def _attn(q, k, vt, q_off, tq, tk):
    b, sq, _ = q.shape
    sk = k.shape[1]
    nq, nk = sq // tq, sk // tk
    qi, ki, last, diag = [], [], [], []
    for i in range(nq):
        q_first, q_last = q_off + i * tq, q_off + i * tq + tq - 1
        j_end = min(nk - 1, (((q_last >> CHUNK_SHIFT) + 1) * CHUNK - 1) // tk)
        for j in range(j_end + 1):
            needs_mask = ((j * tk + tk - 1) >> CHUNK_SHIFT) > (q_first >> CHUNK_SHIFT)
            if needs_mask and not (tq == tk and q_first == j * tk and tk % min(A_PANEL_DIAG, tq) == 0):
                raise NotImplementedError("masked attention tiles must be diagonal and panel aligned")
            qi.append(i)
            ki.append(j)
            last.append(int(j == j_end))
            diag.append(int(needs_mask))
    arrs = [jnp.asarray(np.asarray(a, np.int32)) for a in (qi, ki, last, diag)]
    grid_spec = pltpu.PrefetchScalarGridSpec(
        num_scalar_prefetch=4,
        grid=(b, len(qi)),
        in_specs=[pl.BlockSpec((None, tq, A_KPAD), lambda bi, t, qi, ki, la, dg: (bi, qi[t], 0)),
                  pl.BlockSpec((None, tk, A_KPAD), lambda bi, t, qi, ki, la, dg: (bi, ki[t], 0)),
                  pl.BlockSpec((None, A_WIDTH, tk), lambda bi, t, qi, ki, la, dg: (bi, 0, ki[t]))],
        out_specs=pl.BlockSpec((None, tq, A_WIDTH), lambda bi, t, qi, ki, la, dg: (bi, qi[t], 0)),
        scratch_shapes=[pltpu.VMEM((A_HEADS, 1, tq), F32),
                        pltpu.VMEM((A_HEADS, A_V + A_ONES, tq), F32)],
    )
    return pl.pallas_call(
        functools.partial(_attn_kernel, has_diag=any(diag)),
        grid_spec=grid_spec,
        out_shape=jax.ShapeDtypeStruct((b, sq, A_WIDTH), BF16),
        compiler_params=_params(("parallel", "arbitrary")),
        name="attn",
    )(*arrs, q, k, vt)


def _rope_tables(pos):
    half = A_ROPE // 2
    n = pos.shape[0]
    freq = ROPE_THETA ** (-jnp.arange(half, dtype=F32) / half)
    ang = pos.astype(F32)[:, None] * freq[None, :]
    if (n * half) % LANES == 0:
        ang = ang.reshape(n * half // LANES, LANES)
    cos, sin = lax.optimization_barrier((jnp.cos(ang), jnp.sin(ang)))
    cos, sin = cos.reshape(n, half), sin.reshape(n, half)
    ones = jnp.ones((n, A_NOPE), F32)
    z = lambda w: jnp.zeros((n, w), F32)
    tail = LANES - A_NOPE - A_ROPE
    cos_t = jnp.concatenate([ones, cos, cos, z(tail)], axis=1)
    sin_up = jnp.concatenate([z(A_NOPE + half), sin, z(tail)], axis=1)
    sin_dn = jnp.concatenate([z(A_NOPE), -sin, z(half + tail)], axis=1)
    return cos_t, sin_up, sin_dn


def _pad_heads(w, widths, n_heads):
    kdim = w.shape[0]
    w = w.reshape(kdim, n_heads, sum(widths))
    w = jnp.pad(w, ((0, 0), (0, 0), (0, LANES - sum(widths))))
    return w.reshape(kdim, n_heads * LANES)


def _layer(x, pos, ff1, mix, ff2, w_ukv_k, w_ukv_vt, cache, mlstm_block, attn_tiles):
    b, s, _ = x.shape
    mix_pre_g, w_in_p, bias_row, q_a_g, w_uq_p, kv_a_g, head_g, w_out, mix_post_g = mix
    x1 = _ffn(x.reshape(b * s, D_MODEL), ff1).reshape(b, s, D_MODEL)
    mq, mk, mv, mo, aux, gst, q, c_kv, k_rope = _proj(x1, _rope_tables(pos), mix_pre_g, w_in_p,
                                                      bias_row, q_a_g, w_uq_p, kv_a_g, mlstm_block)
    if cache is None:
        c_all, aux_all, q_off = c_kv, aux, 0
        c0 = jnp.zeros((b, M_HEADS, M_HEAD_DIM, M_HEAD_DIM), F32)
        n0 = jnp.zeros((b, M_HEADS, SUBLANES, M_HEAD_DIM), F32)
        m0 = jnp.zeros((b, SUBLANES, LANES), F32)
    else:
        cache_kv, cache_kr, c0, n0, m0 = cache
        q_off = cache_kv.shape[1]
        c_all = jnp.concatenate([cache_kv, c_kv], axis=1)
        kr_pad = jnp.pad(cache_kr, ((0, 0), (0, 0), (AUX_ROPE_LANE, LANES - AUX_ROPE_LANE - A_ROPE)))
        aux_all = jnp.concatenate([kr_pad, aux], axis=1)
        n0 = jnp.pad(n0[:, :, None, :], ((0, 0), (0, 0), (0, SUBLANES - 1), (0, 0)))
        m0 = jnp.broadcast_to(jnp.pad(m0, ((0, 0), (0, SUBLANES - M_HEADS)))[:, :, None],
                              (b, SUBLANES, LANES))
    k, vt = _expand(c_all, aux_all, w_ukv_k, w_ukv_vt)
    hm, c_new, n_new, m_new = _mlstm(mq, mk, mv, mo, gst, head_g, c0, n0, m0, mlstm_block)
    tq, tk = attn_tiles
    at = _attn(q, k, vt, q_off, tq, min(tk, k.shape[1]))
    n_tok = b * s
    mixer = (hm.reshape(n_tok, M_WIDTH), at.reshape(n_tok, A_WIDTH), w_out, mix_post_g)
    y = _ffn(x1.reshape(n_tok, D_MODEL), ff2, mixer).reshape(b, s, D_MODEL)
    state = (c_kv, k_rope, c_new, n_new[:, :, 0, :], m_new[:, :M_HEADS, 0])
    return y, state


def kernel(x_prompt, x_sample, cache_kv_latent, cache_k_rope, state_C, state_n, state_m, ff1_pre_g, ff1_w_gate, ff1_w_up, ff1_w_down, ff1_post_g, mix_pre_g, w_in, b_igate, b_fgate, q_a_g, w_uq, kv_a_g, w_ukv, m_head_g, w_out, mix_post_g, ff2_pre_g, ff2_w_gate, ff2_w_up, ff2_w_down, ff2_post_g):
    depth = w_in.shape[0]
    y_p, y_s = x_prompt, x_sample
    pos_p = jnp.arange(x_prompt.shape[1])
    pos_s = cache_kv_latent.shape[2] + jnp.arange(x_sample.shape[1])
    new_p, new_s = [], []
    row = lambda g: g.reshape(1, -1)
    for l in range(depth):
        ff1 = (row(ff1_pre_g[l]), ff1_w_gate[l].astype(BF16), ff1_w_up[l].astype(BF16),
               ff1_w_down[l].astype(BF16), row(ff1_post_g[l]))
        ff2 = (row(ff2_pre_g[l]), ff2_w_gate[l].astype(BF16), ff2_w_up[l].astype(BF16),
               ff2_w_down[l].astype(BF16), row(ff2_post_g[l]))
        wl = w_in[l]
        n_m = 4 * M_WIDTH
        gates = wl[:, n_m:n_m + 2 * M_HEADS]
        aq_akv = wl[:, n_m + 2 * M_HEADS:n_m + 2 * M_HEADS + Q_LORA + KV_LORA]
        ar = wl[:, n_m + 2 * M_HEADS + Q_LORA + KV_LORA:]
        zc = lambda w: jnp.zeros((D_MODEL, w), F32)
        aux_cols = jnp.concatenate([gates, zc(AUX_ROPE_LANE - 2 * M_HEADS), ar,
                                    zc(LANES - AUX_ROPE_LANE - A_ROPE)], axis=1)
        w_in_p = jnp.concatenate([wl[:, :n_m], aq_akv, aux_cols], axis=1).astype(BF16)
        bias_row = jnp.concatenate([b_igate[l], b_fgate[l]]).reshape(2 * M_HEADS, 1)
        w_uq_p = _pad_heads(w_uq[l], (A_NOPE, A_ROPE), A_HEADS).astype(BF16)
        wkv = w_ukv[l].reshape(KV_LORA, A_HEADS, A_NOPE + A_V)
        w_ukv_k = _pad_heads(wkv[:, :, :A_NOPE].reshape(KV_LORA, A_HEADS * A_NOPE), (A_NOPE,),
                             A_HEADS).astype(BF16)
        w_ukv_vt = wkv[:, :, A_NOPE:].reshape(KV_LORA, A_WIDTH).T.astype(BF16)
        mix = (row(mix_pre_g[l]), w_in_p, bias_row, row(q_a_g[l]), w_uq_p,
               row(kv_a_g[l]), row(m_head_g[l]), w_out[l].astype(BF16), row(mix_post_g[l]))
        y_p, st_p = _layer(y_p, pos_p, ff1, mix, ff2, w_ukv_k, w_ukv_vt, None,
                           mlstm_block=512, attn_tiles=(1024, 1024))
        cache = (cache_kv_latent[l], cache_k_rope[l], state_C[l], state_n[l], state_m[l])
        y_s, st_s = _layer(y_s, pos_s, ff1, mix, ff2, w_ukv_k, w_ukv_vt, cache,
                           mlstm_block=x_sample.shape[1], attn_tiles=(x_sample.shape[1], 4096))
        new_p.append(st_p)
        new_s.append(st_s)
    outs_p = [jnp.stack(a) for a in zip(*new_p)]
    outs_s = [jnp.stack(a) for a in zip(*new_s)]
    return (y_p, y_s, *outs_p, *outs_s)
```
